```python
import jax, jax.numpy as jnp
from jax import lax
import numpy as np

D_MODEL = 1024
BATCH = 8
SEQ = 2048
DEPTH = 4
DEC_BATCH = 32
DEC_SEQ = 8
PAST_LEN = 8192
PAGE_SIZE = 128

N_HEADS = 8
HEAD_DIM = 64
ATT_WIDTH = N_HEADS * HEAD_DIM
CONV_HEADS = 4
CONV_WIDTH = CONV_HEADS * 64
CONV_K = 3
POOL_WINDOWS = (2, 4, 8, 16)
N_POOL_GROUPS = len(POOL_WINDOWS)
POOL_GROUP = 64
POOL_WIDTH = N_POOL_GROUPS * POOL_GROUP
POOL_MAX = 16
MIX_WIDTH = ATT_WIDTH + CONV_WIDTH + POOL_WIDTH
IN_WIDTH = 3 * ATT_WIDTH + 3 * CONV_WIDTH + POOL_WIDTH
D_FF = 4 * D_MODEL
PLE_DIM = 256
Q_BLOCK = 128
EPS = 1e-6
SB_BIAS_HI = -3.0
SB_BIAS_LO = -9.0

kernel_name = "hymba_style_conv_pool_stickbreaking_step"


def rms_norm(x, g):
    xf = x.astype(jnp.float32)
    y = xf * lax.rsqrt(jnp.mean(xf * xf, axis=-1, keepdims=True) + EPS)
    return (y * g.astype(jnp.float32)).astype(x.dtype)


def stick_breaking(q, k, v, sb_bias, n_past):
    T = q.shape[1]
    scale = HEAD_DIM ** -0.5
    bias = sb_bias.astype(jnp.float32)[None, :, None, None]
    outs = []
    for start in range(0, T, Q_BLOCK):
        end = min(start + Q_BLOCK, T)
        n_keys = n_past + end
        kb = k[:, :n_keys]
        vb = v[:, :n_keys]
        z = jnp.einsum('bqhd,bkhd->bhqk', q[:, start:end], kb,
                       preferred_element_type=jnp.float32) * scale + bias
        t_pos = n_past + jnp.arange(start, end)
        s_pos = jnp.arange(n_keys)
        mask = s_pos[None, :] < t_pos[:, None]
        log_stay = jnp.where(mask, jax.nn.log_sigmoid(-z), 0.0)
        log_between = lax.cumsum(log_stay, axis=3, reverse=True) - log_stay
        a = jnp.where(mask, jnp.exp(jax.nn.log_sigmoid(z) + log_between), 0.0)
        outs.append(jnp.einsum('bhqk,bkhd->bqhd', a.astype(vb.dtype), vb))
    return jnp.concatenate(outs, axis=1)


def multi_scale_pool(ext, x, n_past, pool_w, pool_scale):
    B, T, C = x.shape
    lead = POOL_MAX - 1
    cs = jnp.cumsum(ext.astype(jnp.float32), axis=1)
    cs = jnp.concatenate([jnp.zeros((B, 1, C), jnp.float32), cs], axis=1)
    pos = n_past + jnp.arange(T)
    groups = []
    for g, w in enumerate(POOL_WINDOWS):
        sl = slice(g * POOL_GROUP, (g + 1) * POOL_GROUP)
        win_sum = cs[:, lead + 1:lead + 1 + T, sl] - cs[:, lead + 1 - w:lead + 1 - w + T, sl]
        count = jnp.minimum(pos + 1, w).astype(jnp.float32)[None, :, None]
        groups.append(win_sum / count)
    mean = jnp.concatenate(groups, axis=-1).astype(x.dtype)
    d = (mean - x).reshape(B, T, N_POOL_GROUPS, POOL_GROUP)
    y = jnp.einsum('btgc,gcd->btgd', d, pool_w).reshape(B, T, C)
    return y * pool_scale


def layer(h, p_l, k_past, v_past, conv_buf, pool_buf, norm_mix, w_in, q_norm, k_norm, sb_bias,
          conv_w, conv_b, pool_w, pool_scale, w_out, norm_ffn, w_up, w_down,
          norm_ple, w_ple_gate, w_ple_proj):
    B, T, _ = h.shape
    n_past = k_past.shape[1]
    a = rms_norm(h, norm_mix)
    z = a @ w_in
    o = 0
    q = z[..., o:o + ATT_WIDTH]; o += ATT_WIDTH
    k = z[..., o:o + ATT_WIDTH]; o += ATT_WIDTH
    v = z[..., o:o + ATT_WIDTH]; o += ATT_WIDTH
    c_b = z[..., o:o + CONV_WIDTH]; o += CONV_WIDTH
    c_c = z[..., o:o + CONV_WIDTH]; o += CONV_WIDTH
    c_h = z[..., o:o + CONV_WIDTH]; o += CONV_WIDTH
    u_pool = z[..., o:o + POOL_WIDTH]
    q = rms_norm(q.reshape(B, T, N_HEADS, HEAD_DIM), q_norm)
    k = rms_norm(k.reshape(B, T, N_HEADS, HEAD_DIM), k_norm)
    v = v.reshape(B, T, N_HEADS, HEAD_DIM)
    k_all = jnp.concatenate([k_past, k], axis=1)
    v_all = jnp.concatenate([v_past, v], axis=1)
    y_att = stick_breaking(q, k_all, v_all, sb_bias, n_past).reshape(B, T, ATT_WIDTH)
    u = c_c * c_h
    u_ext = jnp.concatenate([conv_buf, u], axis=1)
    conv = conv_b + sum(u_ext[:, j:j + T] * conv_w[j] for j in range(CONV_K))
    y_conv = c_b * conv
    pool_ext = jnp.concatenate([pool_buf, u_pool], axis=1)
    y_pool = multi_scale_pool(pool_ext, u_pool, n_past, pool_w, pool_scale)
    h = h + jnp.concatenate([y_att, y_conv, y_pool], axis=-1) @ w_out
    m = rms_norm(h, norm_ffn)
    h = h + jnp.square(jax.nn.relu(m @ w_up)) @ w_down
    gate = jax.nn.sigmoid(rms_norm(h, norm_ple) @ w_ple_gate)
    h = h + gate * (p_l @ w_ple_proj)
    return h, k, v, u_ext[:, -(CONV_K - 1):], pool_ext[:, -(POOL_MAX - 1):]


def setup_inputs(seed: int = 0) -> dict:
    key = jax.random.key(seed)
    ks = jax.random.split(key, 25)
    f32 = jnp.float32
    n_pages = PAST_LEN // PAGE_SIZE
    used = DEC_BATCH * n_pages
    n_phys = used + max(1, used // 4)
    nrm = lambda k, s: jax.random.normal(k, s, f32)
    page_table = jax.random.permutation(ks[0], n_phys)[:used].reshape(DEC_BATCH, n_pages).astype(jnp.int32)
    sb_bias = (jnp.linspace(SB_BIAS_HI, SB_BIAS_LO, N_HEADS, dtype=f32)[None, :]
               + 0.1 * nrm(ks[24], (DEPTH, N_HEADS)))
    return {
        "x_prompt": nrm(ks[1], (BATCH, SEQ, D_MODEL)),
        "x_sample": nrm(ks[2], (DEC_BATCH, DEC_SEQ, D_MODEL)),
        "cache_k": nrm(ks[3], (DEPTH, n_phys, PAGE_SIZE, N_HEADS, HEAD_DIM)),
        "cache_v": nrm(ks[4], (DEPTH, n_phys, PAGE_SIZE, N_HEADS, HEAD_DIM)),
        "state_conv": nrm(ks[5], (DEPTH, DEC_BATCH, CONV_K - 1, CONV_WIDTH)),
        "state_pool": nrm(ks[6], (DEPTH, DEC_BATCH, POOL_MAX - 1, POOL_WIDTH)),
        "page_table": page_table,
        "p_prompt": nrm(ks[7], (DEPTH, BATCH, SEQ, PLE_DIM)),
        "p_sample": nrm(ks[8], (DEPTH, DEC_BATCH, DEC_SEQ, PLE_DIM)),
        "norm_mix": 1.0 + 0.02 * nrm(ks[9], (DEPTH, D_MODEL)),
        "w_in": nrm(ks[10], (DEPTH, D_MODEL, IN_WIDTH)) * D_MODEL ** -0.5,
        "q_norm": 1.0 + 0.02 * nrm(ks[11], (DEPTH, HEAD_DIM)),
        "k_norm": 1.0 + 0.02 * nrm(ks[12], (DEPTH, HEAD_DIM)),
        "sb_bias": sb_bias,
        "conv_w": nrm(ks[13], (DEPTH, CONV_K, CONV_WIDTH)) * CONV_K ** -0.5,
        "conv_b": 0.01 * nrm(ks[14], (DEPTH, CONV_WIDTH)),
        "pool_w": nrm(ks[15], (DEPTH, N_POOL_GROUPS, POOL_GROUP, POOL_GROUP)) * POOL_GROUP ** -0.5,
        "pool_scale": 1.0 + 0.02 * nrm(ks[16], (DEPTH, POOL_WIDTH)),
        "w_out": nrm(ks[17], (DEPTH, MIX_WIDTH, D_MODEL)) * MIX_WIDTH ** -0.5,
        "norm_ffn": 1.0 + 0.02 * nrm(ks[18], (DEPTH, D_MODEL)),
        "w_up": nrm(ks[19], (DEPTH, D_MODEL, D_FF)) * D_MODEL ** -0.5,
        "w_down": nrm(ks[20], (DEPTH, D_FF, D_MODEL)) * D_FF ** -0.5,
        "norm_ple": 1.0 + 0.02 * nrm(ks[21], (DEPTH, D_MODEL)),
        "w_ple_gate": nrm(ks[22], (DEPTH, D_MODEL, D_MODEL)) * D_MODEL ** -0.5,
        "w_ple_proj": nrm(ks[23], (DEPTH, PLE_DIM, D_MODEL)) * PLE_DIM ** -0.5,
    }


def reference(x_prompt, x_sample, cache_k, cache_v, state_conv, state_pool, page_table,
              p_prompt, p_sample, norm_mix, w_in, q_norm, k_norm, sb_bias, conv_w, conv_b, pool_w,
              pool_scale, w_out, norm_ffn, w_up, w_down, norm_ple, w_ple_gate, w_ple_proj):
    B, T, _ = x_prompt.shape
    DB, TS, _ = x_sample.shape
    n_pages = page_table.shape[1]
    dt = x_prompt.dtype
    hp, hs = x_prompt, x_sample
    kp_l, vp_l, cp_l, pp_l, ksm_l, vsm_l, csm_l, psm_l = [], [], [], [], [], [], [], []
    for l in range(DEPTH):
        w = (norm_mix[l], w_in[l], q_norm[l], k_norm[l], sb_bias[l], conv_w[l], conv_b[l], pool_w[l],
             pool_scale[l], w_out[l], norm_ffn[l], w_up[l], w_down[l], norm_ple[l],
             w_ple_gate[l], w_ple_proj[l])
        hp, kp, vp, cp, pp = layer(
            hp, p_prompt[l],
            jnp.zeros((B, 0, N_HEADS, HEAD_DIM), dt), jnp.zeros((B, 0, N_HEADS, HEAD_DIM), dt),
            jnp.zeros((B, CONV_K - 1, CONV_WIDTH), dt), jnp.zeros((B, POOL_MAX - 1, POOL_WIDTH), dt),
            *w)
        k_past = cache_k[l][page_table].reshape(DB, n_pages * PAGE_SIZE, N_HEADS, HEAD_DIM)
        v_past = cache_v[l][page_table].reshape(DB, n_pages * PAGE_SIZE, N_HEADS, HEAD_DIM)
        hs, ksm, vsm, csm, psm = layer(hs, p_sample[l], k_past, v_past,
                                       state_conv[l], state_pool[l], *w)
        kp_l.append(kp); vp_l.append(vp); cp_l.append(cp); pp_l.append(pp)
        ksm_l.append(ksm); vsm_l.append(vsm); csm_l.append(csm); psm_l.append(psm)
    return (hp, hs,
            jnp.stack(kp_l), jnp.stack(vp_l), jnp.stack(cp_l), jnp.stack(pp_l),
            jnp.stack(ksm_l), jnp.stack(vsm_l), jnp.stack(csm_l), jnp.stack(psm_l))
```

```python
import functools

import jax
import jax.numpy as jnp
import numpy as np
from jax import lax
from jax.experimental import pallas as pl
from jax.experimental.pallas import tpu as pltpu

D_MODEL = 1024
N_HEADS = 8
HEAD_DIM = 64
ATT_WIDTH = N_HEADS * HEAD_DIM
CONV_WIDTH = 256
CONV_K = 3
POOL_WINDOWS = (2, 4, 8, 16)
POOL_GROUP = 64
POOL_WIDTH = 256
POOL_MAX = 16
IN_WIDTH = 3 * ATT_WIDTH + 3 * CONV_WIDTH + POOL_WIDTH
D_FF = 4 * D_MODEL
PLE_DIM = 256
PAGE_SIZE = 128
EPS = 1e-6

V7X_LANES = 128
V7X_SUBLANES = 8
V7X_VMEM_BYTES = 64 * 1024 * 1024
VMEM_LIMIT_BYTES = V7X_VMEM_BYTES * 7 // 8

CONV_HIST = V7X_SUBLANES
POOL_HIST = POOL_MAX
HEADS_PER_STEP = V7X_LANES // HEAD_DIM
FF_CHUNK = 1024
KEY_BLOCK = 128
QUERY_TILE = 256
PAGES_PER_STEP = 8

BF16 = jnp.bfloat16
F32 = jnp.float32


def _compiler_params(semantics):
    return pltpu.CompilerParams(dimension_semantics=semantics, vmem_limit_bytes=VMEM_LIMIT_BYTES)


def _rms_scale(x):
    return x * lax.rsqrt(jnp.mean(x * x, axis=-1, keepdims=True) + EPS)


def _dot(a, b):
    return jnp.dot(a, b, preferred_element_type=F32)


def _dot_nt(a, b):
    return lax.dot_general(a, b, (((1,), (1,)), ((), ())), preferred_element_type=F32)


def _softplus(z):
    return jnp.maximum(z, 0.0) + jnp.log(1.0 + jnp.exp(-jnp.abs(z)))


def _suffix_sum_matrix():
    r = np.arange(KEY_BLOCK)
    u = (r[None, :] >= r[:, None]).astype(np.float32)
    return jnp.asarray(np.concatenate([u, u], axis=1), dtype=BF16)


def _stick_block(z, valid, carry, u2):
    log_stay = -_softplus(z)
    if valid is not None:
        log_stay = jnp.where(valid, log_stay, 0.0)
    hi = log_stay.astype(BF16)
    lo = (log_stay - hi.astype(F32)).astype(BF16)
    suffix = _dot(u2, jnp.concatenate([hi, lo], axis=0)) + carry
    a = jnp.exp(z + suffix)
    if valid is not None:
        a = jnp.where(valid, a, 0.0)
    return a, suffix[0:1, :]


def _inproj_kernel(h_ref, nm_ref, win_ref, qn_ref, kn_ref, g_ref, cw_ref, cb_ref, pw_ref, ps_ref,
                   cst_ref, pst_ref,
                   q_ref, k_ref, v_ref, kb_ref, vb_ref, cp_ref, cso_ref, pso_ref,
                   uext, pext, *, nb, tm, n_past, n_tiles):
    i = pl.program_id(1)
    rows = nb * tm
    x = h_ref[...].reshape(rows, D_MODEL)
    a = (_rms_scale(x) * nm_ref[...]).astype(BF16)
    z = _dot(a, win_ref[...])

    def head_norm(t, gain):
        ms = _dot((t * t).astype(BF16), g_ref[...])
        return t * lax.rsqrt(ms + EPS) * gain

    q = head_norm(z[:, 0:ATT_WIDTH], qn_ref[...])
    k = head_norm(z[:, ATT_WIDTH:2 * ATT_WIDTH], kn_ref[...])
    v = z[:, 2 * ATT_WIDTH:3 * ATT_WIDTH]
    o = 3 * ATT_WIDTH
    c_b = z[:, o:o + CONV_WIDTH].reshape(nb, tm, CONV_WIDTH)
    c_c = z[:, o + CONV_WIDTH:o + 2 * CONV_WIDTH]
    c_h = z[:, o + 2 * CONV_WIDTH:o + 3 * CONV_WIDTH]
    u_pool = z[:, o + 3 * CONV_WIDTH:o + 3 * CONV_WIDTH + POOL_WIDTH].reshape(nb, tm, POOL_WIDTH)

    q_ref[...] = (q * HEAD_DIM ** -0.5).astype(BF16).reshape(nb, tm, ATT_WIDTH)
    k_ref[...] = k.reshape(nb, tm, ATT_WIDTH)
    v_ref[...] = v.reshape(nb, tm, ATT_WIDTH)
    kb_ref[...] = k.astype(BF16).reshape(nb, tm, ATT_WIDTH)
    vb_ref[...] = v.astype(BF16).reshape(nb, tm, ATT_WIDTH)

    @pl.when(i == 0)
    def _():
        uext[:, 0:CONV_HIST, :] = cst_ref[...]
        pext[:, 0:POOL_HIST, :] = pst_ref[...]

    u = (c_c * c_h).reshape(nb, tm, CONV_WIDTH)
    uext[:, CONV_HIST:CONV_HIST + tm, :] = u
    pext[:, POOL_HIST:POOL_HIST + tm, :] = u_pool

    conv = cb_ref[...] + cw_ref[CONV_K - 1:CONV_K, :] * u
    for j in range(CONV_K - 1):
        back = CONV_K - 1 - j
        conv = conv + cw_ref[j:j + 1, :] * uext[:, CONV_HIST - back:CONV_HIST - back + tm, :]
    y_conv = c_b * conv

    pos = n_past + i * tm + lax.broadcasted_iota(jnp.int32, (nb, tm, V7X_LANES), 1)
    low_group = lax.broadcasted_iota(jnp.int32, (nb, tm, V7X_LANES), 2) < POOL_GROUP
    means = []
    for half in range(POOL_WIDTH // V7X_LANES):
        lanes = slice(half * V7X_LANES, (half + 1) * V7X_LANES)
        w_lo, w_hi = POOL_WINDOWS[2 * half], POOL_WINDOWS[2 * half + 1]
        acc = u_pool[:, :, lanes]
        sum_lo = None
        for back in range(1, w_hi):
            acc = acc + pext[:, POOL_HIST - back:POOL_HIST - back + tm, lanes]
            if back == w_lo - 1:
                sum_lo = acc
        win_sum = jnp.where(low_group, sum_lo, acc)
        count = jnp.minimum(pos + 1, jnp.where(low_group, w_lo, w_hi)).astype(F32)
        means.append(win_sum / count)
    mean = jnp.concatenate(means, axis=-1)
    d = (mean - u_pool).reshape(rows, POOL_WIDTH).astype(BF16)
    y_pool = (_dot(d, pw_ref[...]) * ps_ref[...]).reshape(nb, tm, POOL_WIDTH)

    cp_ref[:, :, 0:CONV_WIDTH] = y_conv.astype(BF16)
    cp_ref[:, :, CONV_WIDTH:CONV_WIDTH + POOL_WIDTH] = y_pool.astype(BF16)

    new_conv_hist = uext[:, tm:tm + CONV_HIST, :]
    new_pool_hist = pext[:, tm:tm + POOL_HIST, :]
    cso_ref[...] = new_conv_hist
    pso_ref[...] = new_pool_hist
    if n_tiles > 1:
        uext[:, 0:CONV_HIST, :] = new_conv_hist
        pext[:, 0:POOL_HIST, :] = new_pool_hist


def _inproj(h, layer, prm, conv_state, pool_state, *, n_past, nb, tm):
    b, t, _ = h.shape
    n_tiles = t // tm
    grid = (b // nb, n_tiles)
    tok = lambda w: pl.BlockSpec((nb, tm, w), lambda bi, i: (bi, i, 0))
    lay = lambda *s: pl.BlockSpec((None,) + s, lambda bi, i: (layer,) + (0,) * len(s))
    hist = lambda r: pl.BlockSpec((nb, r, CONV_WIDTH), lambda bi, i: (bi, 0, 0))
    const = lambda *s: pl.BlockSpec(s, lambda bi, i: (0,) * len(s))
    out_shapes = (
        jax.ShapeDtypeStruct((b, t, ATT_WIDTH), BF16),
        jax.ShapeDtypeStruct((b, t, ATT_WIDTH), F32),
        jax.ShapeDtypeStruct((b, t, ATT_WIDTH), F32),
        jax.ShapeDtypeStruct((b, t, ATT_WIDTH), BF16),
        jax.ShapeDtypeStruct((b, t, ATT_WIDTH), BF16),
        jax.ShapeDtypeStruct((b, t, CONV_WIDTH + POOL_WIDTH), BF16),
        jax.ShapeDtypeStruct((b, CONV_HIST, CONV_WIDTH), F32),
        jax.ShapeDtypeStruct((b, POOL_HIST, POOL_WIDTH), F32),
    )
    return pl.pallas_call(
        functools.partial(_inproj_kernel, nb=nb, tm=tm, n_past=n_past, n_tiles=n_tiles),
        grid=grid,
        in_specs=[tok(D_MODEL), lay(1, D_MODEL), lay(D_MODEL, IN_WIDTH), lay(1, ATT_WIDTH), lay(1, ATT_WIDTH),
                  const(ATT_WIDTH, ATT_WIDTH), lay(CONV_K, CONV_WIDTH), lay(1, CONV_WIDTH),
                  lay(POOL_WIDTH, POOL_WIDTH), lay(1, POOL_WIDTH), hist(CONV_HIST), hist(POOL_HIST)],
        out_specs=(tok(ATT_WIDTH), tok(ATT_WIDTH), tok(ATT_WIDTH), tok(ATT_WIDTH), tok(ATT_WIDTH),
                   tok(CONV_WIDTH + POOL_WIDTH), hist(CONV_HIST), hist(POOL_HIST)),
        out_shape=out_shapes,
        scratch_shapes=[pltpu.VMEM((nb, CONV_HIST + tm, CONV_WIDTH), F32),
                        pltpu.VMEM((nb, POOL_HIST + tm, POOL_WIDTH), F32)],
        compiler_params=_compiler_params(("parallel", "arbitrary")),
        name="inproj",
    )(h, prm["norm_mix"], prm["w_in"], prm["q_norm"], prm["k_norm"], prm["head_mean"], prm["conv_w"],
      prm["conv_b"], prm["pool_w"], prm["pool_scale"], conv_state, pool_state)


def _prompt_attn_kernel(bias_ref, q_ref, k_ref, v_ref, u2_ref, o_ref, vt_ref, *, layer, seq):
    hp = pl.program_id(1)
    qt = pl.program_id(2)
    n_kblocks = seq // KEY_BLOCK
    diag_blocks = QUERY_TILE // KEY_BLOCK

    @pl.when(qt == 0)
    def _():
        for kb in range(n_kblocks):
            vt_ref[kb] = v_ref[kb * KEY_BLOCK:(kb + 1) * KEY_BLOCK, :].T

    q = q_ref[...]
    u2 = u2_ref[...]
    lane_head = lax.broadcasted_iota(jnp.int32, (QUERY_TILE, V7X_LANES), 1) // HEAD_DIM
    key_row = lax.broadcasted_iota(jnp.int32, (KEY_BLOCK, QUERY_TILE), 0)
    query_col = lax.broadcasted_iota(jnp.int32, (KEY_BLOCK, QUERY_TILE), 1)
    n_full = qt * diag_blocks

    outs = []
    for j in range(HEADS_PER_STEP):
        qm = jnp.where(lane_head == j, q, jnp.zeros_like(q))
        bias = bias_ref[layer, hp * HEADS_PER_STEP + j]

        def block(kb, carry, acc, valid, qm=qm, bias=bias, j=j):
            start = pl.multiple_of(kb * KEY_BLOCK, KEY_BLOCK)
            z = _dot_nt(k_ref[pl.ds(start, KEY_BLOCK), :], qm) + bias
            a, carry = _stick_block(z, valid, carry, u2)
            vt = vt_ref[kb, j * HEAD_DIM:(j + 1) * HEAD_DIM, :]
            return carry, acc + _dot(vt, a.astype(BF16))

        carry = jnp.zeros((1, QUERY_TILE), F32)
        acc = jnp.zeros((HEAD_DIM, QUERY_TILE), F32)
        for dblk in reversed(range(diag_blocks)):
            valid = key_row + dblk * KEY_BLOCK < query_col
            carry, acc = block(n_full + dblk, carry, acc, valid)

        def body(it, state):
            return block(n_full - 1 - it, state[0], state[1], None)

        carry, acc = lax.fori_loop(0, n_full, body, (carry, acc))
        outs.append(acc)
    o_ref[...] = jnp.concatenate(outs, axis=0).T.astype(BF16)


def _prompt_attention(q, kb, vb, sb_bias, u2, layer):
    b, t, _ = q.shape
    grid = (b, ATT_WIDTH // V7X_LANES, t // QUERY_TILE)
    return pl.pallas_call(
        functools.partial(_prompt_attn_kernel, layer=layer, seq=t),
        grid=grid,
        in_specs=[pl.BlockSpec(memory_space=pltpu.SMEM),
                  pl.BlockSpec((None, QUERY_TILE, V7X_LANES), lambda bi, hp, qt: (bi, qt, hp)),
                  pl.BlockSpec((None, t, V7X_LANES), lambda bi, hp, qt: (bi, 0, hp)),
                  pl.BlockSpec((None, t, V7X_LANES), lambda bi, hp, qt: (bi, 0, hp)),
                  pl.BlockSpec((KEY_BLOCK, 2 * KEY_BLOCK), lambda bi, hp, qt: (0, 0))],
        out_specs=pl.BlockSpec((None, QUERY_TILE, V7X_LANES), lambda bi, hp, qt: (bi, qt, hp)),
        out_shape=jax.ShapeDtypeStruct((b, t, ATT_WIDTH), BF16),
        scratch_shapes=[pltpu.VMEM((t // KEY_BLOCK, V7X_LANES, KEY_BLOCK), BF16)],
        compiler_params=_compiler_params(("parallel", "parallel", "arbitrary")),
        name="prompt_attn",
    )(sb_bias, q, kb, vb, u2)


def _decode_attn_kernel(pt_ref, q_ref, kn_ref, vn_ref, bias_ref, u2_ref, *rest, dec_seq, n_chunks):
    k_pages = rest[:PAGES_PER_STEP]
    v_pages = rest[PAGES_PER_STEP:2 * PAGES_PER_STEP]
    o_ref = rest[2 * PAGES_PER_STEP]
    qbd_ref, carry_ref, acc_ref = rest[2 * PAGES_PER_STEP + 1:]
    del pt_ref
    c = pl.program_id(1)
    u2 = u2_ref[...]
    bias = bias_ref[...]
    n_cols = N_HEADS * dec_seq

    def block(k_page, v_page, valid):
        z = _dot_nt(k_page.astype(BF16), qbd_ref[...]) + bias
        a, carry = _stick_block(z, valid, carry_ref[...], u2)
        carry_ref[...] = carry
        acc_ref[...] += _dot(a.T.astype(BF16), v_page.astype(BF16))

    @pl.when(c == 0)
    def _():
        row = lax.broadcasted_iota(jnp.int32, (V7X_LANES, ATT_WIDTH), 0)
        lane = lax.broadcasted_iota(jnp.int32, (V7X_LANES, ATT_WIDTH), 1)
        q_rep = jnp.concatenate([q_ref[...].astype(F32)] * (V7X_LANES // dec_seq), axis=0)
        keep = (row < n_cols) & (row // dec_seq == lane // HEAD_DIM)
        qbd_ref[...] = jnp.where(keep, q_rep, 0.0).astype(BF16)
        carry_ref[...] = jnp.zeros_like(carry_ref)
        acc_ref[...] = jnp.zeros_like(acc_ref)
        key_row = lax.broadcasted_iota(jnp.int32, (KEY_BLOCK, V7X_LANES), 0)
        query_t = lax.broadcasted_iota(jnp.int32, (KEY_BLOCK, V7X_LANES), 1) % dec_seq
        block(kn_ref[...], vn_ref[...], key_row < query_t)

    for p in range(PAGES_PER_STEP):
        block(k_pages[p][...], v_pages[p][...], None)

    @pl.when(c == n_chunks - 1)
    def _():
        lane_head = lax.broadcasted_iota(jnp.int32, (dec_seq, ATT_WIDTH), 1) // HEAD_DIM
        out = jnp.zeros((dec_seq, ATT_WIDTH), F32)
        for h in range(N_HEADS):
            out = out + jnp.where(lane_head == h, acc_ref[h * dec_seq:(h + 1) * dec_seq, :], 0.0)
        o_ref[...] = out.astype(BF16)


def _decode_attention(q, k_new, v_new, cache_k, cache_v, page_table, bias_cols, u2, layer):
    db, ts, _ = q.shape
    n_pages = page_table.shape[1]
    n_chunks = n_pages // PAGES_PER_STEP
    assert n_chunks * PAGES_PER_STEP == n_pages and V7X_LANES % ts == 0 and N_HEADS * ts <= V7X_LANES

    def page_spec(p):
        def index(bi, c, pt):
            return (layer, pt[bi, n_pages - 1 - c * PAGES_PER_STEP - p], 0, 0)
        return pl.BlockSpec((None, None, PAGE_SIZE, ATT_WIDTH), index)

    row = lambda r: pl.BlockSpec((None, r, ATT_WIDTH), lambda bi, c, pt: (bi, 0, 0))
    grid_spec = pltpu.PrefetchScalarGridSpec(
        num_scalar_prefetch=1,
        grid=(db, n_chunks),
        in_specs=[row(ts), row(KEY_BLOCK), row(KEY_BLOCK),
                  pl.BlockSpec((None, 1, V7X_LANES), lambda bi, c, pt: (layer, 0, 0)),
                  pl.BlockSpec((KEY_BLOCK, 2 * KEY_BLOCK), lambda bi, c, pt: (0, 0))]
                 + [page_spec(p) for p in range(PAGES_PER_STEP)] * 2,
        out_specs=row(ts),
        scratch_shapes=[pltpu.VMEM((V7X_LANES, ATT_WIDTH), BF16),
                        pltpu.VMEM((1, V7X_LANES), F32),
                        pltpu.VMEM((V7X_LANES, ATT_WIDTH), F32)],
    )
    return pl.pallas_call(
        functools.partial(_decode_attn_kernel, dec_seq=ts, n_chunks=n_chunks),
        grid_spec=grid_spec,
        out_shape=jax.ShapeDtypeStruct((db, ts, ATT_WIDTH), BF16),
        compiler_params=_compiler_params(("parallel", "arbitrary")),
        name="decode_attn",
    )(page_table, q, k_new, v_new, bias_cols, u2, *([cache_k] * PAGES_PER_STEP), *([cache_v] * PAGES_PER_STEP))


def _post_kernel(h_ref, att_ref, cp_ref, p_ref, wo_ref, nf_ref, wu_ref, wd_ref, np_ref, wg_ref, wp_ref, o_ref):
    h = h_ref[...]
    h = h + _dot(att_ref[...], wo_ref[0:ATT_WIDTH, :]) + _dot(cp_ref[...], wo_ref[ATT_WIDTH:D_MODEL, :])
    m = (_rms_scale(h) * nf_ref[...]).astype(BF16)
    ff = jnp.zeros_like(h)
    for c in range(D_FF // FF_CHUNK):
        cols = slice(c * FF_CHUNK, (c + 1) * FF_CHUNK)
        up = jnp.square(jnp.maximum(_dot(m, wu_ref[:, cols]), 0.0)).astype(BF16)
        ff = ff + _dot(up, wd_ref[cols, :])
    h = h + ff
    g = _dot((_rms_scale(h) * np_ref[...]).astype(BF16), wg_ref[...])
    gate = 1.0 / (1.0 + jnp.exp(-g))
    o_ref[...] = h + gate * _dot(p_ref[...].astype(BF16), wp_ref[...])


def _post(h, att, cp, p_all, layer, prm, *, tm):
    n = h.shape[0]
    tok = lambda w: pl.BlockSpec((tm, w), lambda i: (i, 0))
    lay = lambda *s: pl.BlockSpec((None,) + s, lambda i: (layer,) + (0,) * len(s), pipeline_mode=pl.Buffered(1))
    return pl.pallas_call(
        _post_kernel,
        grid=(n // tm,),
        in_specs=[tok(D_MODEL), tok(ATT_WIDTH), tok(CONV_WIDTH + POOL_WIDTH),
                  pl.BlockSpec((None, tm, PLE_DIM), lambda i: (layer, i, 0)),
                  lay(D_MODEL, D_MODEL), lay(1, D_MODEL), lay(D_MODEL, D_FF), lay(D_FF, D_MODEL),
                  lay(1, D_MODEL), lay(D_MODEL, D_MODEL), lay(PLE_DIM, D_MODEL)],
        out_specs=tok(D_MODEL),
        out_shape=jax.ShapeDtypeStruct((n, D_MODEL), F32),
        compiler_params=_compiler_params(("parallel",)),
        name="post",
    )(h, att, cp, p_all, prm["w_out"], prm["norm_ffn"], prm["w_up"], prm["w_down"], prm["norm_ple"],
      prm["w_ple_gate"], prm["w_ple_proj"])


def _tile_rows(n, target):
    tm = min(n, target)
    assert n % tm == 0 and tm % V7X_SUBLANES == 0
    return tm


def kernel(x_prompt, x_sample, cache_k, cache_v, state_conv, state_pool, page_table, p_prompt, p_sample, norm_mix, w_in, q_norm, k_norm, sb_bias, conv_w, conv_b, pool_w, pool_scale, w_out, norm_ffn, w_up, w_down, norm_ple, w_ple_gate, w_ple_proj):
    depth = w_in.shape[0]
    b, t, _ = x_prompt.shape
    db, ts, _ = x_sample.shape
    n_phys = cache_k.shape[1]
    n_past = page_table.shape[1] * PAGE_SIZE
    n_groups = len(POOL_WINDOWS)

    row = lambda a: a.reshape(depth, 1, -1)
    group_eye = jnp.eye(n_groups, dtype=F32)
    head_eye = np.kron(np.eye(N_HEADS, dtype=np.float32), np.full((HEAD_DIM, HEAD_DIM), 1.0 / HEAD_DIM, np.float32))
    prm = {
        "norm_mix": row(norm_mix), "w_in": w_in.astype(BF16),
        "q_norm": row(jnp.tile(q_norm, (1, N_HEADS))), "k_norm": row(jnp.tile(k_norm, (1, N_HEADS))),
        "head_mean": jnp.asarray(head_eye, dtype=BF16),
        "conv_w": conv_w, "conv_b": row(conv_b),
        "pool_w": jnp.einsum("lgcd,gh->lgchd", pool_w, group_eye).reshape(depth, POOL_WIDTH, POOL_WIDTH).astype(BF16),
        "pool_scale": row(pool_scale),
        "w_out": w_out.astype(BF16), "norm_ffn": row(norm_ffn), "w_up": w_up.astype(BF16),
        "w_down": w_down.astype(BF16), "norm_ple": row(norm_ple), "w_ple_gate": w_ple_gate.astype(BF16),
        "w_ple_proj": w_ple_proj.astype(BF16),
    }
    u2 = _suffix_sum_matrix()
    bias_cols = jnp.pad(jnp.repeat(sb_bias, ts, axis=1), ((0, 0), (0, V7X_LANES - N_HEADS * ts))).reshape(depth, 1, V7X_LANES)
    cache_k = cache_k.reshape(depth, n_phys, PAGE_SIZE, ATT_WIDTH)
    cache_v = cache_v.reshape(depth, n_phys, PAGE_SIZE, ATT_WIDTH)
    conv_hist_s = jnp.pad(state_conv, ((0, 0), (0, 0), (CONV_HIST - (CONV_K - 1), 0), (0, 0)))
    pool_hist_s = jnp.pad(state_pool, ((0, 0), (0, 0), (POOL_HIST - (POOL_MAX - 1), 0), (0, 0)))
    conv_hist_p = jnp.zeros((b, CONV_HIST, CONV_WIDTH), F32)
    pool_hist_p = jnp.zeros((b, POOL_HIST, POOL_WIDTH), F32)
    p_prompt = p_prompt.reshape(depth, b * t, PLE_DIM)
    p_sample = p_sample.reshape(depth, db * ts, PLE_DIM)

    tm_p = _tile_rows(t, 512)
    hp, hs = x_prompt, x_sample
    outs = [[] for _ in range(8)]
    for l in range(depth):
        q, k, v, kb, vb, cp, cso, pso = _inproj(hp, l, prm, conv_hist_p, pool_hist_p, n_past=0, nb=1, tm=tm_p)
        att = _prompt_attention(q, kb, vb, sb_bias, u2, l)
        hp = _post(hp.reshape(b * t, D_MODEL), att.reshape(b * t, ATT_WIDTH), cp.reshape(b * t, -1), p_prompt, l,
                   prm, tm=tm_p).reshape(b, t, D_MODEL)
        for lst, val in zip(outs[:4], (k, v, cso, pso)):
            lst.append(val)

        q, k, v, kb, vb, cp, cso, pso = _inproj(hs, l, prm, conv_hist_s[l], pool_hist_s[l], n_past=n_past, nb=db, tm=ts)
        pad_rows = ((0, 0), (0, KEY_BLOCK - ts), (0, 0))
        att = _decode_attention(q, jnp.pad(kb, pad_rows), jnp.pad(vb, pad_rows), cache_k, cache_v, page_table,
                                bias_cols, u2, l)
        hs = _post(hs.reshape(db * ts, D_MODEL), att.reshape(db * ts, ATT_WIDTH), cp.reshape(db * ts, -1), p_sample, l,
                   prm, tm=db * ts).reshape(db, ts, D_MODEL)
        for lst, val in zip(outs[4:], (k, v, cso, pso)):
            lst.append(val)

    def heads(xs):
        s = jnp.stack(xs)
        return s.reshape(s.shape[:-1] + (N_HEADS, HEAD_DIM))

    conv_tail = lambda xs: jnp.stack(xs)[:, :, CONV_HIST - (CONV_K - 1):, :]
    pool_tail = lambda xs: jnp.stack(xs)[:, :, POOL_HIST - (POOL_MAX - 1):, :]
    return (hp, hs,
            heads(outs[0]), heads(outs[1]), conv_tail(outs[2]), pool_tail(outs[3]),
            heads(outs[4]), heads(outs[5]), conv_tail(outs[6]), pool_tail(outs[7]))
```

```python
import functools

import jax
import jax.numpy as jnp
import numpy as np
from jax import lax
from jax.experimental import pallas as pl
from jax.experimental.pallas import tpu as pltpu

D_MODEL = 1024
N_HEADS = 8
HEAD_DIM = 64
ATT_WIDTH = N_HEADS * HEAD_DIM
CONV_WIDTH = 256
CONV_K = 3
POOL_WINDOWS = (2, 4, 8, 16)
POOL_GROUP = 64
POOL_WIDTH = 256
POOL_MAX = 16
IN_WIDTH = 3 * ATT_WIDTH + 3 * CONV_WIDTH + POOL_WIDTH
D_FF = 4 * D_MODEL
PLE_DIM = 256
PAGE_SIZE = 128
EPS = 1e-6

V7X_LANES = 128
V7X_SUBLANES = 8
V7X_VMEM_BYTES = 64 * 1024 * 1024
VMEM_LIMIT_BYTES = V7X_VMEM_BYTES * 7 // 8

CONV_HIST = V7X_SUBLANES
POOL_HIST = POOL_MAX
HEADS_PER_VREG = V7X_LANES // HEAD_DIM
FF_CHUNK = 1024
KEY_BLOCK = 128
QUERY_TILE = 256
PAGES_PER_STEP = 16
PAGES_PER_GROUP = 4
MASKED_LOGIT = -1e30

BF16 = jnp.bfloat16
F32 = jnp.float32


def _compiler_params(semantics):
    return pltpu.CompilerParams(dimension_semantics=semantics, vmem_limit_bytes=VMEM_LIMIT_BYTES)


def _rms_scale(x):
    return x * lax.rsqrt(jnp.mean(x * x, axis=-1, keepdims=True) + EPS)


def _dot(a, b):
    return jnp.dot(a, b, preferred_element_type=F32)


def _dot_nt(a, b):
    return lax.dot_general(a, b, (((1,), (1,)), ((), ())), preferred_element_type=F32)


def _softplus(z):
    return jnp.maximum(z, 0.0) + jnp.log(1.0 + jnp.exp(-jnp.abs(z)))


def _suffix_sum_matrices():
    r = np.arange(KEY_BLOCK)
    u = (r[None, :] >= r[:, None]).astype(np.float32)
    rows = np.concatenate([u, u], axis=1)
    half = np.concatenate([u.T, np.ones_like(u)], axis=1)
    lanes = np.concatenate([half, half], axis=0)
    return jnp.asarray(rows, dtype=BF16), jnp.asarray(lanes, dtype=BF16)


def _log_stay_split(z, valid):
    log_stay = -_softplus(z)
    if valid is not None:
        log_stay = jnp.where(valid, log_stay, 0.0)
    hi = log_stay.astype(BF16)
    lo = (log_stay - hi.astype(F32)).astype(BF16)
    return hi, lo


def _stick_weights(z, suffix, valid):
    a = jnp.exp(z + suffix)
    if valid is not None:
        a = jnp.where(valid, a, 0.0)
    return a


def _stick_block_rows(z, valid, carry, u2):
    hi, lo = _log_stay_split(z, valid)
    suffix = _dot(u2, jnp.concatenate([hi, lo], axis=0))
    if carry is not None:
        suffix = suffix + carry
    return _stick_weights(z, suffix, valid), suffix[0:1, :]


def _stick_block_lanes(z, valid, carry, ut2):
    hi, lo = _log_stay_split(z, valid)
    sums = _dot(jnp.concatenate([hi, lo], axis=1), ut2)
    suffix = sums[:, 0:KEY_BLOCK] + carry
    return _stick_weights(z, suffix, valid), carry + sums[:, KEY_BLOCK:2 * KEY_BLOCK]


def _inproj_kernel(h_ref, nm_ref, win_ref, qn_ref, kn_ref, g_ref, cw_ref, cb_ref, pw_ref, ps_ref,
                   cst_ref, pst_ref,
                   q_ref, k_ref, v_ref, kb_ref, vb_ref, cp_ref, cso_ref, pso_ref,
                   uext, pext, *, nb, tm, n_past, n_tiles):
    i = pl.program_id(1)
    rows = nb * tm
    x = h_ref[...].reshape(rows, D_MODEL)
    a = (_rms_scale(x) * nm_ref[...]).astype(BF16)
    z = _dot(a, win_ref[...])

    def head_norm(t, gain):
        ms = _dot((t * t).astype(BF16), g_ref[...])
        return t * lax.rsqrt(ms + EPS) * gain

    q = head_norm(z[:, 0:ATT_WIDTH], qn_ref[...])
    k = head_norm(z[:, ATT_WIDTH:2 * ATT_WIDTH], kn_ref[...])
    v = z[:, 2 * ATT_WIDTH:3 * ATT_WIDTH]
    o = 3 * ATT_WIDTH
    c_b = z[:, o:o + CONV_WIDTH].reshape(nb, tm, CONV_WIDTH)
    c_c = z[:, o + CONV_WIDTH:o + 2 * CONV_WIDTH]
    c_h = z[:, o + 2 * CONV_WIDTH:o + 3 * CONV_WIDTH]
    u_pool = z[:, o + 3 * CONV_WIDTH:o + 3 * CONV_WIDTH + POOL_WIDTH].reshape(nb, tm, POOL_WIDTH)

    q_ref[...] = (q * HEAD_DIM ** -0.5).astype(BF16).reshape(nb, tm, ATT_WIDTH)
    k_ref[...] = k.reshape(nb, tm, ATT_WIDTH)
    v_ref[...] = v.reshape(nb, tm, ATT_WIDTH)
    kb_ref[...] = k.astype(BF16).reshape(nb, tm, ATT_WIDTH)
    vb_ref[...] = v.astype(BF16).reshape(nb, tm, ATT_WIDTH)

    @pl.when(i == 0)
    def _():
        uext[:, 0:CONV_HIST, :] = cst_ref[...]
        pext[:, 0:POOL_HIST, :] = pst_ref[...]

    u = (c_c * c_h).reshape(nb, tm, CONV_WIDTH)
    uext[:, CONV_HIST:CONV_HIST + tm, :] = u
    pext[:, POOL_HIST:POOL_HIST + tm, :] = u_pool

    conv = cb_ref[...] + cw_ref[CONV_K - 1:CONV_K, :] * u
    for j in range(CONV_K - 1):
        back = CONV_K - 1 - j
        conv = conv + cw_ref[j:j + 1, :] * uext[:, CONV_HIST - back:CONV_HIST - back + tm, :]
    y_conv = c_b * conv

    pos = n_past + i * tm + lax.broadcasted_iota(jnp.int32, (nb, tm, V7X_LANES), 1)
    low_group = lax.broadcasted_iota(jnp.int32, (nb, tm, V7X_LANES), 2) < POOL_GROUP
    means = []
    for half in range(POOL_WIDTH // V7X_LANES):
        lanes = slice(half * V7X_LANES, (half + 1) * V7X_LANES)
        w_lo, w_hi = POOL_WINDOWS[2 * half], POOL_WINDOWS[2 * half + 1]
        acc = u_pool[:, :, lanes]
        sum_lo = None
        for back in range(1, w_hi):
            acc = acc + pext[:, POOL_HIST - back:POOL_HIST - back + tm, lanes]
            if back == w_lo - 1:
                sum_lo = acc
        win_sum = jnp.where(low_group, sum_lo, acc)
        count = jnp.minimum(pos + 1, jnp.where(low_group, w_lo, w_hi)).astype(F32)
        means.append(win_sum / count)
    mean = jnp.concatenate(means, axis=-1)
    d = (mean - u_pool).reshape(rows, POOL_WIDTH).astype(BF16)
    y_pool = (_dot(d, pw_ref[...]) * ps_ref[...]).reshape(nb, tm, POOL_WIDTH)

    cp_ref[:, :, 0:CONV_WIDTH] = y_conv.astype(BF16)
    cp_ref[:, :, CONV_WIDTH:CONV_WIDTH + POOL_WIDTH] = y_pool.astype(BF16)

    new_conv_hist = uext[:, tm:tm + CONV_HIST, :]
    new_pool_hist = pext[:, tm:tm + POOL_HIST, :]
    cso_ref[...] = new_conv_hist
    pso_ref[...] = new_pool_hist
    if n_tiles > 1:
        uext[:, 0:CONV_HIST, :] = new_conv_hist
        pext[:, 0:POOL_HIST, :] = new_pool_hist


def _inproj(h, layer, prm, conv_state, pool_state, *, n_past, nb, tm):
    b, t, _ = h.shape
    n_tiles = t // tm
    grid = (b // nb, n_tiles)
    tok = lambda w: pl.BlockSpec((nb, tm, w), lambda bi, i: (bi, i, 0))
    lay = lambda *s: pl.BlockSpec((None,) + s, lambda bi, i: (layer,) + (0,) * len(s))
    hist = lambda r: pl.BlockSpec((nb, r, CONV_WIDTH), lambda bi, i: (bi, 0, 0))
    const = lambda *s: pl.BlockSpec(s, lambda bi, i: (0,) * len(s))
    out_shapes = (
        jax.ShapeDtypeStruct((b, t, ATT_WIDTH), BF16),
        jax.ShapeDtypeStruct((b, t, ATT_WIDTH), F32),
        jax.ShapeDtypeStruct((b, t, ATT_WIDTH), F32),
        jax.ShapeDtypeStruct((b, t, ATT_WIDTH), BF16),
        jax.ShapeDtypeStruct((b, t, ATT_WIDTH), BF16),
        jax.ShapeDtypeStruct((b, t, CONV_WIDTH + POOL_WIDTH), BF16),
        jax.ShapeDtypeStruct((b, CONV_HIST, CONV_WIDTH), F32),
        jax.ShapeDtypeStruct((b, POOL_HIST, POOL_WIDTH), F32),
    )
    return pl.pallas_call(
        functools.partial(_inproj_kernel, nb=nb, tm=tm, n_past=n_past, n_tiles=n_tiles),
        grid=grid,
        in_specs=[tok(D_MODEL), lay(1, D_MODEL), lay(D_MODEL, IN_WIDTH), lay(1, ATT_WIDTH), lay(1, ATT_WIDTH),
                  const(ATT_WIDTH, ATT_WIDTH), lay(CONV_K, CONV_WIDTH), lay(1, CONV_WIDTH),
                  lay(POOL_WIDTH, POOL_WIDTH), lay(1, POOL_WIDTH), hist(CONV_HIST), hist(POOL_HIST)],
        out_specs=(tok(ATT_WIDTH), tok(ATT_WIDTH), tok(ATT_WIDTH), tok(ATT_WIDTH), tok(ATT_WIDTH),
                   tok(CONV_WIDTH + POOL_WIDTH), hist(CONV_HIST), hist(POOL_HIST)),
        out_shape=out_shapes,
        scratch_shapes=[pltpu.VMEM((nb, CONV_HIST + tm, CONV_WIDTH), F32),
                        pltpu.VMEM((nb, POOL_HIST + tm, POOL_WIDTH), F32)],
        compiler_params=_compiler_params(("parallel", "arbitrary")),
        name="inproj",
    )(h, prm["norm_mix"], prm["w_in"], prm["q_norm"], prm["k_norm"], prm["head_mean"], prm["conv_w"],
      prm["conv_b"], prm["pool_w"], prm["pool_scale"], conv_state, pool_state)


def _prompt_attn_kernel(bias_ref, q_ref, k_ref, v_ref, u2_ref, o_ref, vt_ref, qm_ref, carry_ref, acc_ref,
                        z_ref, hl_ref, a_ref, *, layer, seq):
    qt = pl.program_id(1)
    n_kblocks = seq // KEY_BLOCK
    diag_blocks = QUERY_TILE // KEY_BLOCK

    @pl.when(qt == 0)
    def _():
        for kb in range(n_kblocks):
            vt_ref[kb] = v_ref[kb * KEY_BLOCK:(kb + 1) * KEY_BLOCK, :].T

    lane_head = lax.broadcasted_iota(jnp.int32, (QUERY_TILE, V7X_LANES), 1) // HEAD_DIM
    for h in range(N_HEADS):
        pair = slice(h // HEADS_PER_VREG * V7X_LANES, (h // HEADS_PER_VREG + 1) * V7X_LANES)
        q_pair = q_ref[:, pair]
        qm_ref[h] = jnp.where(lane_head == h % HEADS_PER_VREG, q_pair, jnp.zeros_like(q_pair))

    acc_ref[...] = jnp.zeros_like(acc_ref)
    u2 = u2_ref[...]
    n_full = qt * diag_blocks
    last = n_full + diag_blocks - 1

    def logits(kb, slot, valid):
        start = pl.multiple_of(kb * KEY_BLOCK, KEY_BLOCK)
        for h in range(N_HEADS):
            pair = slice(h // HEADS_PER_VREG * V7X_LANES, (h // HEADS_PER_VREG + 1) * V7X_LANES)
            z = _dot_nt(k_ref[pl.ds(start, KEY_BLOCK), pair], qm_ref[h]) + bias_ref[layer, h]
            hi, lo = _log_stay_split(z, valid)
            hl_ref[slot, h, 0:KEY_BLOCK, :] = hi
            hl_ref[slot, h, KEY_BLOCK:2 * KEY_BLOCK, :] = lo
            z_ref[slot, h] = z if valid is None else jnp.where(valid, z, MASKED_LOGIT)

    def weights(slot, first):
        for h in range(N_HEADS):
            suffix = _dot(u2, hl_ref[slot, h])
            if not first:
                suffix = suffix + carry_ref[h]
            a_ref[slot, h] = jnp.exp(z_ref[slot, h] + suffix).astype(BF16)
            carry_ref[h] = suffix[0:1, :]

    def values(kb, slot):
        for h in range(N_HEADS):
            acc_ref[h] += _dot(vt_ref[kb, h * HEAD_DIM:(h + 1) * HEAD_DIM, :], a_ref[slot, h])

    key_row = lax.broadcasted_iota(jnp.int32, (KEY_BLOCK, QUERY_TILE), 0)
    query_col = lax.broadcasted_iota(jnp.int32, (KEY_BLOCK, QUERY_TILE), 1)
    assert diag_blocks == 2
    logits(last, 0, key_row + KEY_BLOCK < query_col)
    logits(last - 1, 1, key_row < query_col)
    weights(0, first=True)

    def body(j, _):
        newest = last - 2 * j
        logits(newest - 2, 0, None)
        weights(1, first=False)
        values(newest, 0)
        logits(newest - 3, 1, None)
        weights(0, first=False)
        values(newest - 1, 1)
        return 0

    lax.fori_loop(0, qt, body, 0)
    weights(1, first=False)
    values(1, 0)
    values(0, 1)
    for hp in range(N_HEADS // HEADS_PER_VREG):
        both = jnp.concatenate([acc_ref[hp * HEADS_PER_VREG + j] for j in range(HEADS_PER_VREG)], axis=0)
        o_ref[:, hp * V7X_LANES:(hp + 1) * V7X_LANES] = both.T.astype(BF16)


def _prompt_attention(q, kb, vb, sb_bias, u2, layer):
    b, t, _ = q.shape
    seq_block = pl.BlockSpec((None, t, ATT_WIDTH), lambda bi, qt: (bi, 0, 0))
    tile = pl.BlockSpec((None, QUERY_TILE, ATT_WIDTH), lambda bi, qt: (bi, qt, 0))
    return pl.pallas_call(
        functools.partial(_prompt_attn_kernel, layer=layer, seq=t),
        grid=(b, t // QUERY_TILE),
        in_specs=[pl.BlockSpec(memory_space=pltpu.SMEM), tile, seq_block, seq_block,
                  pl.BlockSpec((KEY_BLOCK, 2 * KEY_BLOCK), lambda bi, qt: (0, 0))],
        out_specs=tile,
        out_shape=jax.ShapeDtypeStruct((b, t, ATT_WIDTH), BF16),
        scratch_shapes=[pltpu.VMEM((t // KEY_BLOCK, ATT_WIDTH, KEY_BLOCK), BF16),
                        pltpu.VMEM((N_HEADS, QUERY_TILE, V7X_LANES), BF16),
                        pltpu.VMEM((N_HEADS, 1, QUERY_TILE), F32),
                        pltpu.VMEM((N_HEADS, HEAD_DIM, QUERY_TILE), F32),
                        pltpu.VMEM((2, N_HEADS, KEY_BLOCK, QUERY_TILE), F32),
                        pltpu.VMEM((2, N_HEADS, 2 * KEY_BLOCK, QUERY_TILE), BF16),
                        pltpu.VMEM((2, N_HEADS, KEY_BLOCK, QUERY_TILE), BF16)],
        compiler_params=_compiler_params(("parallel", "arbitrary")),
        name="prompt_attn",
    )(sb_bias, q, kb, vb, u2)


def _decode_attn_kernel(pt_ref, q_ref, kn_ref, vn_ref, bias_ref, ut2_ref, *rest, dec_seq, n_chunks):
    k_pages = rest[:PAGES_PER_STEP]
    v_pages = rest[PAGES_PER_STEP:2 * PAGES_PER_STEP]
    o_ref = rest[2 * PAGES_PER_STEP]
    qbd_ref, carry_ref, acc_ref = rest[2 * PAGES_PER_STEP + 1:]
    del pt_ref
    c = pl.program_id(1)
    n_rows = N_HEADS * dec_seq
    ut2 = ut2_ref[...]
    bias = bias_ref[...]

    def block(z, valid, weigh):
        a, carry = _stick_block_lanes(z + bias, valid, carry_ref[...], ut2)
        carry_ref[...] = carry
        acc_ref[...] += weigh(a.astype(BF16))

    @pl.when(c == 0)
    def _():
        row_head = lax.broadcasted_iota(jnp.int32, (n_rows, ATT_WIDTH), 0) // dec_seq
        lane_head = lax.broadcasted_iota(jnp.int32, (n_rows, ATT_WIDTH), 1) // HEAD_DIM
        q_rep = jnp.concatenate([q_ref[...].astype(F32)] * N_HEADS, axis=0)
        qbd_ref[...] = jnp.where(row_head == lane_head, q_rep, 0.0).astype(BF16)
        carry_ref[...] = jnp.zeros_like(carry_ref)
        acc_ref[...] = jnp.zeros_like(acc_ref)
        key_pos = lax.broadcasted_iota(jnp.int32, (n_rows, KEY_BLOCK), 1)
        query_t = lax.broadcasted_iota(jnp.int32, (n_rows, KEY_BLOCK), 0) % dec_seq
        block(_dot_nt(qbd_ref[...], kn_ref[...]), key_pos < query_t, lambda a: _dot(a, vn_ref[...]))

    for g in range(0, PAGES_PER_STEP, PAGES_PER_GROUP):
        group = range(g, g + PAGES_PER_GROUP)
        zs = [_dot(qbd_ref[...], k_pages[p][...].astype(BF16)) + bias for p in group]
        splits = [_log_stay_split(z, None) for z in zs]
        sums = _dot(jnp.concatenate([jnp.concatenate(s, axis=1) for s in splits], axis=0), ut2)
        carry = carry_ref[...]
        contrib = None
        for j, p in enumerate(group):
            rows = slice(j * n_rows, (j + 1) * n_rows)
            a = _stick_weights(zs[j], sums[rows, 0:KEY_BLOCK] + carry, None).astype(BF16)
            carry = carry + sums[rows, KEY_BLOCK:2 * KEY_BLOCK]
            part = _dot_nt(a, v_pages[p][...].astype(BF16))
            contrib = part if contrib is None else contrib + part
        carry_ref[...] = carry
        acc_ref[...] += contrib

    @pl.when(c == n_chunks - 1)
    def _():
        lane_head = lax.broadcasted_iota(jnp.int32, (dec_seq, ATT_WIDTH), 1) // HEAD_DIM
        out = jnp.zeros((dec_seq, ATT_WIDTH), F32)
        for h in range(N_HEADS):
            out = out + jnp.where(lane_head == h, acc_ref[h * dec_seq:(h + 1) * dec_seq, :], 0.0)
        o_ref[...] = out.astype(BF16)


def _decode_attention(q, k_new, v_new, cache_kt, cache_vt, page_table, bias_rows, ut2, layer):
    db, ts, _ = q.shape
    n_pages = page_table.shape[1]
    n_chunks = n_pages // PAGES_PER_STEP
    n_rows = N_HEADS * ts
    assert n_chunks * PAGES_PER_STEP == n_pages and ts % V7X_SUBLANES == 0 and ts <= KEY_BLOCK

    def page_spec(p):
        def index(bi, c, pt):
            return (layer, pt[bi, n_pages - 1 - c * PAGES_PER_STEP - p], 0, 0)
        return pl.BlockSpec((None, None, ATT_WIDTH, PAGE_SIZE), index)

    row = lambda r: pl.BlockSpec((None, r, ATT_WIDTH), lambda bi, c, pt: (bi, 0, 0))
    grid_spec = pltpu.PrefetchScalarGridSpec(
        num_scalar_prefetch=1,
        grid=(db, n_chunks),
        in_specs=[row(ts), row(KEY_BLOCK), row(KEY_BLOCK),
                  pl.BlockSpec((None, n_rows, KEY_BLOCK), lambda bi, c, pt: (layer, 0, 0)),
                  pl.BlockSpec((2 * KEY_BLOCK, 2 * KEY_BLOCK), lambda bi, c, pt: (0, 0))]
                 + [page_spec(p) for p in range(PAGES_PER_STEP)] * 2,
        out_specs=row(ts),
        scratch_shapes=[pltpu.VMEM((n_rows, ATT_WIDTH), BF16),
                        pltpu.VMEM((n_rows, KEY_BLOCK), F32),
                        pltpu.VMEM((n_rows, ATT_WIDTH), F32)],
    )
    return pl.pallas_call(
        functools.partial(_decode_attn_kernel, dec_seq=ts, n_chunks=n_chunks),
        grid_spec=grid_spec,
        out_shape=jax.ShapeDtypeStruct((db, ts, ATT_WIDTH), BF16),
        compiler_params=_compiler_params(("parallel", "arbitrary")),
        name="decode_attn",
    )(page_table, q, k_new, v_new, bias_rows, ut2, *([cache_kt] * PAGES_PER_STEP), *([cache_vt] * PAGES_PER_STEP))


def _post_kernel(h_ref, att_ref, cp_ref, p_ref, wo_ref, nf_ref, wu_ref, wd_ref, np_ref, wg_ref, wp_ref, o_ref):
    h = h_ref[...]
    h = h + _dot(att_ref[...], wo_ref[0:ATT_WIDTH, :]) + _dot(cp_ref[...], wo_ref[ATT_WIDTH:D_MODEL, :])
    m = (_rms_scale(h) * nf_ref[...]).astype(BF16)
    ff = jnp.zeros_like(h)
    for c in range(D_FF // FF_CHUNK):
        cols = slice(c * FF_CHUNK, (c + 1) * FF_CHUNK)
        up = jnp.square(jnp.maximum(_dot(m, wu_ref[:, cols]), 0.0)).astype(BF16)
        ff = ff + _dot(up, wd_ref[cols, :])
    h = h + ff
    g = _dot((_rms_scale(h) * np_ref[...]).astype(BF16), wg_ref[...])
    gate = 1.0 / (1.0 + jnp.exp(-g))
    o_ref[...] = h + gate * _dot(p_ref[...].astype(BF16), wp_ref[...])


def _post(h, att, cp, p_all, layer, prm, *, tm):
    n = h.shape[0]
    tok = lambda w: pl.BlockSpec((tm, w), lambda i: (i, 0))
    lay = lambda *s: pl.BlockSpec((None,) + s, lambda i: (layer,) + (0,) * len(s), pipeline_mode=pl.Buffered(1))
    return pl.pallas_call(
        _post_kernel,
        grid=(n // tm,),
        in_specs=[tok(D_MODEL), tok(ATT_WIDTH), tok(CONV_WIDTH + POOL_WIDTH),
                  pl.BlockSpec((None, tm, PLE_DIM), lambda i: (layer, i, 0)),
                  lay(D_MODEL, D_MODEL), lay(1, D_MODEL), lay(D_MODEL, D_FF), lay(D_FF, D_MODEL),
                  lay(1, D_MODEL), lay(D_MODEL, D_MODEL), lay(PLE_DIM, D_MODEL)],
        out_specs=tok(D_MODEL),
        out_shape=jax.ShapeDtypeStruct((n, D_MODEL), F32),
        compiler_params=_compiler_params(("parallel",)),
        name="post",
    )(h, att, cp, p_all, prm["w_out"], prm["norm_ffn"], prm["w_up"], prm["w_down"], prm["norm_ple"],
      prm["w_ple_gate"], prm["w_ple_proj"])


def _tile_rows(n, target):
    tm = min(n, target)
    assert n % tm == 0 and tm % V7X_SUBLANES == 0
    return tm


def kernel(x_prompt, x_sample, cache_k, cache_v, state_conv, state_pool, page_table, p_prompt, p_sample, norm_mix, w_in, q_norm, k_norm, sb_bias, conv_w, conv_b, pool_w, pool_scale, w_out, norm_ffn, w_up, w_down, norm_ple, w_ple_gate, w_ple_proj):
    depth = w_in.shape[0]
    b, t, _ = x_prompt.shape
    db, ts, _ = x_sample.shape
    n_phys = cache_k.shape[1]
    n_past = page_table.shape[1] * PAGE_SIZE
    n_groups = len(POOL_WINDOWS)

    row = lambda a: a.reshape(depth, 1, -1)
    group_eye = jnp.eye(n_groups, dtype=F32)
    head_eye = np.kron(np.eye(N_HEADS, dtype=np.float32), np.full((HEAD_DIM, HEAD_DIM), 1.0 / HEAD_DIM, np.float32))
    prm = {
        "norm_mix": row(norm_mix), "w_in": w_in.astype(BF16),
        "q_norm": row(jnp.tile(q_norm, (1, N_HEADS))), "k_norm": row(jnp.tile(k_norm, (1, N_HEADS))),
        "head_mean": jnp.asarray(head_eye, dtype=BF16),
        "conv_w": conv_w, "conv_b": row(conv_b),
        "pool_w": jnp.einsum("lgcd,gh->lgchd", pool_w, group_eye).reshape(depth, POOL_WIDTH, POOL_WIDTH).astype(BF16),
        "pool_scale": row(pool_scale),
        "w_out": w_out.astype(BF16), "norm_ffn": row(norm_ffn), "w_up": w_up.astype(BF16),
        "w_down": w_down.astype(BF16), "norm_ple": row(norm_ple), "w_ple_gate": w_ple_gate.astype(BF16),
        "w_ple_proj": w_ple_proj.astype(BF16),
    }
    u2, ut2 = _suffix_sum_matrices()
    bias_rows = jnp.broadcast_to(jnp.repeat(sb_bias, ts, axis=1)[:, :, None], (depth, N_HEADS * ts, KEY_BLOCK))
    cache_kt = jnp.transpose(cache_k, (0, 1, 3, 4, 2)).reshape(depth, n_phys, ATT_WIDTH, PAGE_SIZE)
    cache_vt = jnp.transpose(cache_v, (0, 1, 3, 4, 2)).reshape(depth, n_phys, ATT_WIDTH, PAGE_SIZE)
    conv_hist_s = jnp.pad(state_conv, ((0, 0), (0, 0), (CONV_HIST - (CONV_K - 1), 0), (0, 0)))
    pool_hist_s = jnp.pad(state_pool, ((0, 0), (0, 0), (POOL_HIST - (POOL_MAX - 1), 0), (0, 0)))
    conv_hist_p = jnp.zeros((b, CONV_HIST, CONV_WIDTH), F32)
    pool_hist_p = jnp.zeros((b, POOL_HIST, POOL_WIDTH), F32)
    p_prompt = p_prompt.reshape(depth, b * t, PLE_DIM)
    p_sample = p_sample.reshape(depth, db * ts, PLE_DIM)

    tm_p = _tile_rows(t, 512)
    hp, hs = x_prompt, x_sample
    outs = [[] for _ in range(8)]
    for l in range(depth):
        q, k, v, kb, vb, cp, cso, pso = _inproj(hp, l, prm, conv_hist_p, pool_hist_p, n_past=0, nb=1, tm=tm_p)
        att = _prompt_attention(q, kb, vb, sb_bias, u2, l)
        hp = _post(hp.reshape(b * t, D_MODEL), att.reshape(b * t, ATT_WIDTH), cp.reshape(b * t, -1), p_prompt, l,
                   prm, tm=tm_p).reshape(b, t, D_MODEL)
        for lst, val in zip(outs[:4], (k, v, cso, pso)):
            lst.append(val)

        q, k, v, kb, vb, cp, cso, pso = _inproj(hs, l, prm, conv_hist_s[l], pool_hist_s[l], n_past=n_past, nb=db, tm=ts)
        pad_rows = ((0, 0), (0, KEY_BLOCK - ts), (0, 0))
        att = _decode_attention(q, jnp.pad(kb, pad_rows), jnp.pad(vb, pad_rows), cache_kt, cache_vt, page_table,
                                bias_rows, ut2, l)
        hs = _post(hs.reshape(db * ts, D_MODEL), att.reshape(db * ts, ATT_WIDTH), cp.reshape(db * ts, -1), p_sample, l,
                   prm, tm=db * ts).reshape(db, ts, D_MODEL)
        for lst, val in zip(outs[4:], (k, v, cso, pso)):
            lst.append(val)

    def heads(xs):
        s = jnp.stack(xs)
        return s.reshape(s.shape[:-1] + (N_HEADS, HEAD_DIM))

    conv_tail = lambda xs: jnp.stack(xs)[:, :, CONV_HIST - (CONV_K - 1):, :]
    pool_tail = lambda xs: jnp.stack(xs)[:, :, POOL_HIST - (POOL_MAX - 1):, :]
    return (hp, hs,
            heads(outs[0]), heads(outs[1]), conv_tail(outs[2]), pool_tail(outs[3]),
            heads(outs[4]), heads(outs[5]), conv_tail(outs[6]), pool_tail(outs[7]))
```

```python
import functools

import jax
import jax.numpy as jnp
import numpy as np
from jax import lax
from jax.experimental import pallas as pl
from jax.experimental.pallas import tpu as pltpu

D_MODEL = 1024
N_HEADS = 8
HEAD_DIM = 64
ATT_WIDTH = N_HEADS * HEAD_DIM
CONV_WIDTH = 256
CONV_K = 3
POOL_WINDOWS = (2, 4, 8, 16)
POOL_GROUP = 64
POOL_WIDTH = 256
POOL_MAX = 16
IN_WIDTH = 3 * ATT_WIDTH + 3 * CONV_WIDTH + POOL_WIDTH
D_FF = 4 * D_MODEL
PLE_DIM = 256
PAGE_SIZE = 128
EPS = 1e-6

V7X_LANES = 128
V7X_SUBLANES = 8
V7X_VMEM_BYTES = 64 * 1024 * 1024
VMEM_LIMIT_BYTES = V7X_VMEM_BYTES * 7 // 8

CONV_HIST = V7X_SUBLANES
POOL_HIST = POOL_MAX
HEADS_PER_VREG = V7X_LANES // HEAD_DIM
FF_CHUNK = 1024
KEY_BLOCK = 128
QUERY_TILE = 256
PAGES_PER_STEP = 16
PAGES_PER_GROUP = 4
MASKED_LOGIT = -1e30

BF16 = jnp.bfloat16
F32 = jnp.float32


def _compiler_params(semantics):
    return pltpu.CompilerParams(dimension_semantics=semantics, vmem_limit_bytes=VMEM_LIMIT_BYTES)


def _rms_scale(x):
    return x * lax.rsqrt(jnp.mean(x * x, axis=-1, keepdims=True) + EPS)


def _dot(a, b):
    return jnp.dot(a, b, preferred_element_type=F32)


def _dot_nt(a, b):
    return lax.dot_general(a, b, (((1,), (1,)), ((), ())), preferred_element_type=F32)


def _softplus(z):
    sign_bit = jnp.uint32(0x80000000)
    neg_abs = lax.bitcast_convert_type(lax.bitcast_convert_type(z, jnp.uint32) | sign_bit, F32)
    return jnp.maximum(z, 0.0) + jnp.log(1.0 + jnp.exp(neg_abs))


def _suffix_sum_matrices():
    r = np.arange(KEY_BLOCK)
    u = -(r[None, :] >= r[:, None]).astype(np.float32)
    lanes = np.concatenate([u.T, -np.ones_like(u)], axis=1)
    return jnp.asarray(u, dtype=BF16), jnp.asarray(lanes, dtype=BF16)


def _neg_log_stay(z, valid):
    s = _softplus(z)
    if valid is not None:
        s = jnp.where(valid, s, 0.0)
    return s.astype(BF16)


def _stick_weights(z, suffix, valid):
    a = jnp.exp(z + suffix)
    if valid is not None:
        a = jnp.where(valid, a, 0.0)
    return a


def _stick_block_lanes(z, valid, carry, ut):
    sums = _dot(_neg_log_stay(z, valid), ut)
    suffix = sums[:, 0:KEY_BLOCK] + carry
    return _stick_weights(z, suffix, valid), carry + sums[:, KEY_BLOCK:2 * KEY_BLOCK]


def _inproj_kernel(h_ref, nm_ref, win_ref, qn_ref, kn_ref, g_ref, cw_ref, cb_ref, pw_ref, ps_ref,
                   cst_ref, pst_ref,
                   q_ref, k_ref, v_ref, kb_ref, vb_ref, cp_ref, cso_ref, pso_ref,
                   uext, pext, *, nb, tm, n_past, n_tiles):
    i = pl.program_id(1)
    rows = nb * tm
    x = h_ref[...].reshape(rows, D_MODEL)
    a = (_rms_scale(x) * nm_ref[...]).astype(BF16)
    z = _dot(a, win_ref[...])

    def head_norm(t, gain):
        ms = _dot((t * t).astype(BF16), g_ref[...])
        return t * lax.rsqrt(ms + EPS) * gain

    q = head_norm(z[:, 0:ATT_WIDTH], qn_ref[...])
    k = head_norm(z[:, ATT_WIDTH:2 * ATT_WIDTH], kn_ref[...])
    v = z[:, 2 * ATT_WIDTH:3 * ATT_WIDTH]
    o = 3 * ATT_WIDTH
    c_b = z[:, o:o + CONV_WIDTH].reshape(nb, tm, CONV_WIDTH)
    c_c = z[:, o + CONV_WIDTH:o + 2 * CONV_WIDTH]
    c_h = z[:, o + 2 * CONV_WIDTH:o + 3 * CONV_WIDTH]
    u_pool = z[:, o + 3 * CONV_WIDTH:o + 3 * CONV_WIDTH + POOL_WIDTH].reshape(nb, tm, POOL_WIDTH)

    q_ref[...] = (q * HEAD_DIM ** -0.5).astype(BF16).reshape(nb, tm, ATT_WIDTH)
    k_ref[...] = k.reshape(nb, tm, ATT_WIDTH)
    v_ref[...] = v.reshape(nb, tm, ATT_WIDTH)
    kb_ref[...] = k.astype(BF16).reshape(nb, tm, ATT_WIDTH)
    vb_ref[...] = v.astype(BF16).reshape(nb, tm, ATT_WIDTH)

    @pl.when(i == 0)
    def _():
        uext[:, 0:CONV_HIST, :] = cst_ref[...]
        pext[:, 0:POOL_HIST, :] = pst_ref[...]

    u = (c_c * c_h).reshape(nb, tm, CONV_WIDTH)
    uext[:, CONV_HIST:CONV_HIST + tm, :] = u
    pext[:, POOL_HIST:POOL_HIST + tm, :] = u_pool

    conv = cb_ref[...] + cw_ref[CONV_K - 1:CONV_K, :] * u
    for j in range(CONV_K - 1):
        back = CONV_K - 1 - j
        conv = conv + cw_ref[j:j + 1, :] * uext[:, CONV_HIST - back:CONV_HIST - back + tm, :]
    y_conv = c_b * conv

    pos = n_past + i * tm + lax.broadcasted_iota(jnp.int32, (nb, tm, V7X_LANES), 1)
    low_group = lax.broadcasted_iota(jnp.int32, (nb, tm, V7X_LANES), 2) < POOL_GROUP
    means = []
    for half in range(POOL_WIDTH // V7X_LANES):
        lanes = slice(half * V7X_LANES, (half + 1) * V7X_LANES)
        w_lo, w_hi = POOL_WINDOWS[2 * half], POOL_WINDOWS[2 * half + 1]
        acc = u_pool[:, :, lanes]
        sum_lo = None
        for back in range(1, w_hi):
            acc = acc + pext[:, POOL_HIST - back:POOL_HIST - back + tm, lanes]
            if back == w_lo - 1:
                sum_lo = acc
        win_sum = jnp.where(low_group, sum_lo, acc)
        count = jnp.minimum(pos + 1, jnp.where(low_group, w_lo, w_hi)).astype(F32)
        means.append(win_sum / count)
    mean = jnp.concatenate(means, axis=-1)
    d = (mean - u_pool).reshape(rows, POOL_WIDTH).astype(BF16)
    y_pool = (_dot(d, pw_ref[...]) * ps_ref[...]).reshape(nb, tm, POOL_WIDTH)

    cp_ref[:, :, 0:CONV_WIDTH] = y_conv.astype(BF16)
    cp_ref[:, :, CONV_WIDTH:CONV_WIDTH + POOL_WIDTH] = y_pool.astype(BF16)

    new_conv_hist = uext[:, tm:tm + CONV_HIST, :]
    new_pool_hist = pext[:, tm:tm + POOL_HIST, :]
    cso_ref[...] = new_conv_hist
    pso_ref[...] = new_pool_hist
    if n_tiles > 1:
        uext[:, 0:CONV_HIST, :] = new_conv_hist
        pext[:, 0:POOL_HIST, :] = new_pool_hist


def _inproj(h, layer, prm, conv_state, pool_state, *, n_past, nb, tm):
    b, t, _ = h.shape
    n_tiles = t // tm
    grid = (b // nb, n_tiles)
    tok = lambda w: pl.BlockSpec((nb, tm, w), lambda bi, i: (bi, i, 0))
    lay = lambda *s: pl.BlockSpec((None,) + s, lambda bi, i: (layer,) + (0,) * len(s))
    hist = lambda r: pl.BlockSpec((nb, r, CONV_WIDTH), lambda bi, i: (bi, 0, 0))
    const = lambda *s: pl.BlockSpec(s, lambda bi, i: (0,) * len(s))
    out_shapes = (
        jax.ShapeDtypeStruct((b, t, ATT_WIDTH), BF16),
        jax.ShapeDtypeStruct((b, t, ATT_WIDTH), F32),
        jax.ShapeDtypeStruct((b, t, ATT_WIDTH), F32),
        jax.ShapeDtypeStruct((b, t, ATT_WIDTH), BF16),
        jax.ShapeDtypeStruct((b, t, ATT_WIDTH), BF16),
        jax.ShapeDtypeStruct((b, t, CONV_WIDTH + POOL_WIDTH), BF16),
        jax.ShapeDtypeStruct((b, CONV_HIST, CONV_WIDTH), F32),
        jax.ShapeDtypeStruct((b, POOL_HIST, POOL_WIDTH), F32),
    )
    return pl.pallas_call(
        functools.partial(_inproj_kernel, nb=nb, tm=tm, n_past=n_past, n_tiles=n_tiles),
        grid=grid,
        in_specs=[tok(D_MODEL), lay(1, D_MODEL), lay(D_MODEL, IN_WIDTH), lay(1, ATT_WIDTH), lay(1, ATT_WIDTH),
                  const(ATT_WIDTH, ATT_WIDTH), lay(CONV_K, CONV_WIDTH), lay(1, CONV_WIDTH),
                  lay(POOL_WIDTH, POOL_WIDTH), lay(1, POOL_WIDTH), hist(CONV_HIST), hist(POOL_HIST)],
        out_specs=(tok(ATT_WIDTH), tok(ATT_WIDTH), tok(ATT_WIDTH), tok(ATT_WIDTH), tok(ATT_WIDTH),
                   tok(CONV_WIDTH + POOL_WIDTH), hist(CONV_HIST), hist(POOL_HIST)),
        out_shape=out_shapes,
        scratch_shapes=[pltpu.VMEM((nb, CONV_HIST + tm, CONV_WIDTH), F32),
                        pltpu.VMEM((nb, POOL_HIST + tm, POOL_WIDTH), F32)],
        compiler_params=_compiler_params(("parallel", "arbitrary")),
        name="inproj",
    )(h, prm["norm_mix"], prm["w_in"], prm["q_norm"], prm["k_norm"], prm["head_mean"], prm["conv_w"],
      prm["conv_b"], prm["pool_w"], prm["pool_scale"], conv_state, pool_state)


def _prompt_attn_kernel(bias_ref, q_ref, k_ref, v_ref, u_ref, o_ref, vt_ref, qm_ref, carry_ref, acc_ref,
                        z_ref, s_ref, a_ref, *, layer, seq):
    qt = pl.program_id(1)
    n_kblocks = seq // KEY_BLOCK
    diag_blocks = QUERY_TILE // KEY_BLOCK

    @pl.when(qt == 0)
    def _():
        for kb in range(n_kblocks):
            vt_ref[kb] = v_ref[kb * KEY_BLOCK:(kb + 1) * KEY_BLOCK, :].T

    lane_head = lax.broadcasted_iota(jnp.int32, (QUERY_TILE, V7X_LANES), 1) // HEAD_DIM
    for h in range(N_HEADS):
        pair = slice(h // HEADS_PER_VREG * V7X_LANES, (h // HEADS_PER_VREG + 1) * V7X_LANES)
        q_pair = q_ref[:, pair]
        qm_ref[h] = jnp.where(lane_head == h % HEADS_PER_VREG, q_pair, jnp.zeros_like(q_pair))

    acc_ref[...] = jnp.zeros_like(acc_ref)
    u = u_ref[...]
    n_full = qt * diag_blocks
    last = n_full + diag_blocks - 1

    def logits(kb, slot, valid):
        start = pl.multiple_of(kb * KEY_BLOCK, KEY_BLOCK)
        for h in range(N_HEADS):
            pair = slice(h // HEADS_PER_VREG * V7X_LANES, (h // HEADS_PER_VREG + 1) * V7X_LANES)
            z = _dot_nt(k_ref[pl.ds(start, KEY_BLOCK), pair], qm_ref[h]) + bias_ref[layer, h]
            s_ref[slot, h] = _neg_log_stay(z, valid)
            z_ref[slot, h] = z if valid is None else jnp.where(valid, z, MASKED_LOGIT)

    def weights(slot, first):
        for h in range(N_HEADS):
            suffix = _dot(u, s_ref[slot, h])
            if not first:
                suffix = suffix + carry_ref[h]
            a_ref[slot, h] = jnp.exp(z_ref[slot, h] + suffix).astype(BF16)
            carry_ref[h] = suffix[0:1, :]

    def values(kb, slot):
        for h in range(N_HEADS):
            acc_ref[h] += _dot(vt_ref[kb, h * HEAD_DIM:(h + 1) * HEAD_DIM, :], a_ref[slot, h])

    key_row = lax.broadcasted_iota(jnp.int32, (KEY_BLOCK, QUERY_TILE), 0)
    query_col = lax.broadcasted_iota(jnp.int32, (KEY_BLOCK, QUERY_TILE), 1)
    assert diag_blocks == 2
    logits(last, 0, key_row + KEY_BLOCK < query_col)
    logits(last - 1, 1, key_row < query_col)
    weights(0, first=True)

    def body(j, _):
        newest = last - 2 * j
        logits(newest - 2, 0, None)
        weights(1, first=False)
        values(newest, 0)
        logits(newest - 3, 1, None)
        weights(0, first=False)
        values(newest - 1, 1)
        return 0

    lax.fori_loop(0, qt, body, 0)
    weights(1, first=False)
    values(1, 0)
    values(0, 1)
    for hp in range(N_HEADS // HEADS_PER_VREG):
        both = jnp.concatenate([acc_ref[hp * HEADS_PER_VREG + j] for j in range(HEADS_PER_VREG)], axis=0)
        o_ref[:, hp * V7X_LANES:(hp + 1) * V7X_LANES] = both.T.astype(BF16)


def _prompt_attention(q, kb, vb, sb_bias, u, layer):
    b, t, _ = q.shape
    seq_block = pl.BlockSpec((None, t, ATT_WIDTH), lambda bi, qt: (bi, 0, 0))
    tile = pl.BlockSpec((None, QUERY_TILE, ATT_WIDTH), lambda bi, qt: (bi, qt, 0))
    return pl.pallas_call(
        functools.partial(_prompt_attn_kernel, layer=layer, seq=t),
        grid=(b, t // QUERY_TILE),
        in_specs=[pl.BlockSpec(memory_space=pltpu.SMEM), tile, seq_block, seq_block,
                  pl.BlockSpec((KEY_BLOCK, KEY_BLOCK), lambda bi, qt: (0, 0))],
        out_specs=tile,
        out_shape=jax.ShapeDtypeStruct((b, t, ATT_WIDTH), BF16),
        scratch_shapes=[pltpu.VMEM((t // KEY_BLOCK, ATT_WIDTH, KEY_BLOCK), BF16),
                        pltpu.VMEM((N_HEADS, QUERY_TILE, V7X_LANES), BF16),
                        pltpu.VMEM((N_HEADS, 1, QUERY_TILE), F32),
                        pltpu.VMEM((N_HEADS, HEAD_DIM, QUERY_TILE), F32),
                        pltpu.VMEM((2, N_HEADS, KEY_BLOCK, QUERY_TILE), F32),
                        pltpu.VMEM((2, N_HEADS, KEY_BLOCK, QUERY_TILE), BF16),
                        pltpu.VMEM((2, N_HEADS, KEY_BLOCK, QUERY_TILE), BF16)],
        compiler_params=_compiler_params(("parallel", "arbitrary")),
        name="prompt_attn",
    )(sb_bias, q, kb, vb, u)


def _decode_attn_kernel(pt_ref, q_ref, kn_ref, vn_ref, bias_ref, ut_ref, *rest, dec_seq, n_chunks):
    k_pages = rest[:PAGES_PER_STEP]
    v_pages = rest[PAGES_PER_STEP:2 * PAGES_PER_STEP]
    o_ref = rest[2 * PAGES_PER_STEP]
    qbd_ref, carry_ref, acc_ref, z_ref, s_ref, a_ref = rest[2 * PAGES_PER_STEP + 1:]
    del pt_ref
    c = pl.program_id(1)
    n_rows = N_HEADS * dec_seq
    ut = ut_ref[...]
    bias = bias_ref[...]

    def block(z, valid, weigh):
        a, carry = _stick_block_lanes(z + bias, valid, carry_ref[...], ut)
        carry_ref[...] = carry
        acc_ref[...] += weigh(a.astype(BF16))

    @pl.when(c == 0)
    def _():
        row_head = lax.broadcasted_iota(jnp.int32, (n_rows, ATT_WIDTH), 0) // dec_seq
        lane_head = lax.broadcasted_iota(jnp.int32, (n_rows, ATT_WIDTH), 1) // HEAD_DIM
        q_rep = jnp.concatenate([q_ref[...].astype(F32)] * N_HEADS, axis=0)
        qbd_ref[...] = jnp.where(row_head == lane_head, q_rep, 0.0).astype(BF16)
        carry_ref[...] = jnp.zeros_like(carry_ref)
        acc_ref[...] = jnp.zeros_like(acc_ref)
        key_pos = lax.broadcasted_iota(jnp.int32, (n_rows, KEY_BLOCK), 1)
        query_t = lax.broadcasted_iota(jnp.int32, (n_rows, KEY_BLOCK), 0) % dec_seq
        block(_dot_nt(qbd_ref[...], kn_ref[...]), key_pos < query_t, lambda a: _dot(a, vn_ref[...]))

    for p in range(PAGES_PER_STEP):
        z = _dot(qbd_ref[...], k_pages[p][...].astype(BF16)) + bias
        z_ref[p] = z
        s_ref[p * n_rows:(p + 1) * n_rows, :] = _neg_log_stay(z, None)
    carry = carry_ref[...]
    for g in range(0, PAGES_PER_STEP, PAGES_PER_GROUP):
        sums = _dot(s_ref[g * n_rows:(g + PAGES_PER_GROUP) * n_rows, :], ut)
        for j in range(PAGES_PER_GROUP):
            rows = slice(j * n_rows, (j + 1) * n_rows)
            a_ref[g + j] = _stick_weights(z_ref[g + j], sums[rows, 0:KEY_BLOCK] + carry, None).astype(BF16)
            carry = carry + sums[rows, KEY_BLOCK:2 * KEY_BLOCK]
    carry_ref[...] = carry
    contrib = acc_ref[...]
    for p in range(PAGES_PER_STEP):
        contrib = contrib + _dot_nt(a_ref[p], v_pages[p][...].astype(BF16))
    acc_ref[...] = contrib

    @pl.when(c == n_chunks - 1)
    def _():
        lane_head = lax.broadcasted_iota(jnp.int32, (dec_seq, ATT_WIDTH), 1) // HEAD_DIM
        out = jnp.zeros((dec_seq, ATT_WIDTH), F32)
        for h in range(N_HEADS):
            out = out + jnp.where(lane_head == h, acc_ref[h * dec_seq:(h + 1) * dec_seq, :], 0.0)
        o_ref[...] = out.astype(BF16)


def _decode_attention(q, k_new, v_new, cache_kt, cache_vt, page_table, bias_rows, ut, layer):
    db, ts, _ = q.shape
    n_pages = page_table.shape[1]
    n_chunks = n_pages // PAGES_PER_STEP
    n_rows = N_HEADS * ts
    assert n_chunks * PAGES_PER_STEP == n_pages and ts % V7X_SUBLANES == 0 and ts <= KEY_BLOCK

    def page_spec(p):
        def index(bi, c, pt):
            return (layer, pt[bi, n_pages - 1 - c * PAGES_PER_STEP - p], 0, 0)
        return pl.BlockSpec((None, None, ATT_WIDTH, PAGE_SIZE), index)

    row = lambda r: pl.BlockSpec((None, r, ATT_WIDTH), lambda bi, c, pt: (bi, 0, 0))
    grid_spec = pltpu.PrefetchScalarGridSpec(
        num_scalar_prefetch=1,
        grid=(db, n_chunks),
        in_specs=[row(ts), row(KEY_BLOCK), row(KEY_BLOCK),
                  pl.BlockSpec((None, n_rows, KEY_BLOCK), lambda bi, c, pt: (layer, 0, 0)),
                  pl.BlockSpec((KEY_BLOCK, 2 * KEY_BLOCK), lambda bi, c, pt: (0, 0))]
                 + [page_spec(p) for p in range(PAGES_PER_STEP)] * 2,
        out_specs=row(ts),
        scratch_shapes=[pltpu.VMEM((n_rows, ATT_WIDTH), BF16),
                        pltpu.VMEM((n_rows, KEY_BLOCK), F32),
                        pltpu.VMEM((n_rows, ATT_WIDTH), F32),
                        pltpu.VMEM((PAGES_PER_STEP, n_rows, KEY_BLOCK), F32),
                        pltpu.VMEM((PAGES_PER_STEP * n_rows, KEY_BLOCK), BF16),
                        pltpu.VMEM((PAGES_PER_STEP, n_rows, KEY_BLOCK), BF16)],
    )
    return pl.pallas_call(
        functools.partial(_decode_attn_kernel, dec_seq=ts, n_chunks=n_chunks),
        grid_spec=grid_spec,
        out_shape=jax.ShapeDtypeStruct((db, ts, ATT_WIDTH), BF16),
        compiler_params=_compiler_params(("parallel", "arbitrary")),
        name="decode_attn",
    )(page_table, q, k_new, v_new, bias_rows, ut, *([cache_kt] * PAGES_PER_STEP), *([cache_vt] * PAGES_PER_STEP))


def _post_kernel(h_ref, att_ref, cp_ref, p_ref, wo_ref, nf_ref, wu_ref, wd_ref, np_ref, wg_ref, wp_ref, o_ref):
    h = h_ref[...]
    h = h + _dot(att_ref[...], wo_ref[0:ATT_WIDTH, :]) + _dot(cp_ref[...], wo_ref[ATT_WIDTH:D_MODEL, :])
    m = (_rms_scale(h) * nf_ref[...]).astype(BF16)
    ff = jnp.zeros_like(h)
    for c in range(D_FF // FF_CHUNK):
        cols = slice(c * FF_CHUNK, (c + 1) * FF_CHUNK)
        up = jnp.square(jnp.maximum(_dot(m, wu_ref[:, cols]), 0.0)).astype(BF16)
        ff = ff + _dot(up, wd_ref[cols, :])
    h = h + ff
    g = _dot((_rms_scale(h) * np_ref[...]).astype(BF16), wg_ref[...])
    gate = 1.0 / (1.0 + jnp.exp(-g))
    o_ref[...] = h + gate * _dot(p_ref[...].astype(BF16), wp_ref[...])


def _post(h, att, cp, p_all, layer, prm, *, tm):
    n = h.shape[0]
    tok = lambda w: pl.BlockSpec((tm, w), lambda i: (i, 0))
    lay = lambda *s: pl.BlockSpec((None,) + s, lambda i: (layer,) + (0,) * len(s), pipeline_mode=pl.Buffered(1))
    return pl.pallas_call(
        _post_kernel,
        grid=(n // tm,),
        in_specs=[tok(D_MODEL), tok(ATT_WIDTH), tok(CONV_WIDTH + POOL_WIDTH),
                  pl.BlockSpec((None, tm, PLE_DIM), lambda i: (layer, i, 0)),
                  lay(D_MODEL, D_MODEL), lay(1, D_MODEL), lay(D_MODEL, D_FF), lay(D_FF, D_MODEL),
                  lay(1, D_MODEL), lay(D_MODEL, D_MODEL), lay(PLE_DIM, D_MODEL)],
        out_specs=tok(D_MODEL),
        out_shape=jax.ShapeDtypeStruct((n, D_MODEL), F32),
        compiler_params=_compiler_params(("parallel",)),
        name="post",
    )(h, att, cp, p_all, prm["w_out"], prm["norm_ffn"], prm["w_up"], prm["w_down"], prm["norm_ple"],
      prm["w_ple_gate"], prm["w_ple_proj"])


def _tile_rows(n, target):
    tm = min(n, target)
    assert n % tm == 0 and tm % V7X_SUBLANES == 0
    return tm


def kernel(x_prompt, x_sample, cache_k, cache_v, state_conv, state_pool, page_table, p_prompt, p_sample, norm_mix, w_in, q_norm, k_norm, sb_bias, conv_w, conv_b, pool_w, pool_scale, w_out, norm_ffn, w_up, w_down, norm_ple, w_ple_gate, w_ple_proj):
    depth = w_in.shape[0]
    b, t, _ = x_prompt.shape
    db, ts, _ = x_sample.shape
    n_phys = cache_k.shape[1]
    n_past = page_table.shape[1] * PAGE_SIZE
    n_groups = len(POOL_WINDOWS)

    row = lambda a: a.reshape(depth, 1, -1)
    group_eye = jnp.eye(n_groups, dtype=F32)
    head_eye = np.kron(np.eye(N_HEADS, dtype=np.float32), np.full((HEAD_DIM, HEAD_DIM), 1.0 / HEAD_DIM, np.float32))
    prm = {
        "norm_mix": row(norm_mix), "w_in": w_in.astype(BF16),
        "q_norm": row(jnp.tile(q_norm, (1, N_HEADS))), "k_norm": row(jnp.tile(k_norm, (1, N_HEADS))),
        "head_mean": jnp.asarray(head_eye, dtype=BF16),
        "conv_w": conv_w, "conv_b": row(conv_b),
        "pool_w": jnp.einsum("lgcd,gh->lgchd", pool_w, group_eye).reshape(depth, POOL_WIDTH, POOL_WIDTH).astype(BF16),
        "pool_scale": row(pool_scale),
        "w_out": w_out.astype(BF16), "norm_ffn": row(norm_ffn), "w_up": w_up.astype(BF16),
        "w_down": w_down.astype(BF16), "norm_ple": row(norm_ple), "w_ple_gate": w_ple_gate.astype(BF16),
        "w_ple_proj": w_ple_proj.astype(BF16),
    }
    u, ut = _suffix_sum_matrices()
    bias_rows = jnp.broadcast_to(jnp.repeat(sb_bias, ts, axis=1)[:, :, None], (depth, N_HEADS * ts, KEY_BLOCK))
    cache_kt = jnp.transpose(cache_k, (0, 1, 3, 4, 2)).reshape(depth, n_phys, ATT_WIDTH, PAGE_SIZE)
    cache_vt = jnp.transpose(cache_v, (0, 1, 3, 4, 2)).reshape(depth, n_phys, ATT_WIDTH, PAGE_SIZE)
    conv_hist_s = jnp.pad(state_conv, ((0, 0), (0, 0), (CONV_HIST - (CONV_K - 1), 0), (0, 0)))
    pool_hist_s = jnp.pad(state_pool, ((0, 0), (0, 0), (POOL_HIST - (POOL_MAX - 1), 0), (0, 0)))
    conv_hist_p = jnp.zeros((b, CONV_HIST, CONV_WIDTH), F32)
    pool_hist_p = jnp.zeros((b, POOL_HIST, POOL_WIDTH), F32)
    p_prompt = p_prompt.reshape(depth, b * t, PLE_DIM)
    p_sample = p_sample.reshape(depth, db * ts, PLE_DIM)

    tm_p = _tile_rows(t, 512)
    hp, hs = x_prompt, x_sample
    outs = [[] for _ in range(8)]
    for l in range(depth):
        q, k, v, kb, vb, cp, cso, pso = _inproj(hp, l, prm, conv_hist_p, pool_hist_p, n_past=0, nb=1, tm=tm_p)
        att = _prompt_attention(q, kb, vb, sb_bias, u, l)
        hp = _post(hp.reshape(b * t, D_MODEL), att.reshape(b * t, ATT_WIDTH), cp.reshape(b * t, -1), p_prompt, l,
                   prm, tm=tm_p).reshape(b, t, D_MODEL)
        for lst, val in zip(outs[:4], (k, v, cso, pso)):
            lst.append(val)

        q, k, v, kb, vb, cp, cso, pso = _inproj(hs, l, prm, conv_hist_s[l], pool_hist_s[l], n_past=n_past, nb=db, tm=ts)
        pad_rows = ((0, 0), (0, KEY_BLOCK - ts), (0, 0))
        att = _decode_attention(q, jnp.pad(kb, pad_rows), jnp.pad(vb, pad_rows), cache_kt, cache_vt, page_table,
                                bias_rows, ut, l)
        hs = _post(hs.reshape(db * ts, D_MODEL), att.reshape(db * ts, ATT_WIDTH), cp.reshape(db * ts, -1), p_sample, l,
                   prm, tm=db * ts).reshape(db, ts, D_MODEL)
        for lst, val in zip(outs[4:], (k, v, cso, pso)):
            lst.append(val)

    def heads(xs):
        s = jnp.stack(xs)
        return s.reshape(s.shape[:-1] + (N_HEADS, HEAD_DIM))

    conv_tail = lambda xs: jnp.stack(xs)[:, :, CONV_HIST - (CONV_K - 1):, :]
    pool_tail = lambda xs: jnp.stack(xs)[:, :, POOL_HIST - (POOL_MAX - 1):, :]
    return (hp, hs,
            heads(outs[0]), heads(outs[1]), conv_tail(outs[2]), pool_tail(outs[3]),
            heads(outs[4]), heads(outs[5]), conv_tail(outs[6]), pool_tail(outs[7]))
```

```python
import functools

import jax
import jax.numpy as jnp
import numpy as np
from jax import lax
from jax.experimental import pallas as pl
from jax.experimental.pallas import tpu as pltpu

D_MODEL = 1024
N_HEADS = 8
HEAD_DIM = 64
ATT_WIDTH = N_HEADS * HEAD_DIM
CONV_WIDTH = 256
CONV_K = 3
POOL_WINDOWS = (2, 4, 8, 16)
POOL_GROUP = 64
POOL_WIDTH = 256
POOL_MAX = 16
IN_WIDTH = 3 * ATT_WIDTH + 3 * CONV_WIDTH + POOL_WIDTH
D_FF = 4 * D_MODEL
PLE_DIM = 256
PAGE_SIZE = 128
EPS = 1e-6

V7X_LANES = 128
V7X_SUBLANES = 8
V7X_VMEM_BYTES = 64 * 1024 * 1024
VMEM_LIMIT_BYTES = V7X_VMEM_BYTES * 7 // 8

CONV_HIST = V7X_SUBLANES
POOL_HIST = POOL_MAX
HEADS_PER_VREG = V7X_LANES // HEAD_DIM
FF_CHUNK = 1024
KEY_BLOCK = 128
QUERY_TILE = 256
PAGES_PER_STEP = 32
PAGES_PER_GROUP = 4
MASKED_LOGIT = -1e30
LOG2_E = 1.4426950408889634

BF16 = jnp.bfloat16
F32 = jnp.float32


def _compiler_params(semantics):
    return pltpu.CompilerParams(dimension_semantics=semantics, vmem_limit_bytes=VMEM_LIMIT_BYTES)


def _rms_scale(x):
    return x * lax.rsqrt(jnp.mean(x * x, axis=-1, keepdims=True) + EPS)


def _dot(a, b):
    return jnp.dot(a, b, preferred_element_type=F32)


def _dot_nt(a, b):
    return lax.dot_general(a, b, (((1,), (1,)), ((), ())), preferred_element_type=F32)


def _softplus(z):
    return jnp.maximum(z, 0.0) + jnp.log(1.0 + jnp.exp2(jnp.abs(z) * -LOG2_E))


def _suffix_sum_matrices():
    r = np.arange(KEY_BLOCK)
    u = -(r[None, :] >= r[:, None]).astype(np.float32)
    lanes = np.concatenate([u.T, -np.ones_like(u)], axis=1)
    return jnp.asarray(u, dtype=BF16), jnp.asarray(lanes, dtype=BF16)


def _neg_log_stay(z, valid):
    s = _softplus(z)
    if valid is not None:
        s = jnp.where(valid, s, 0.0)
    return s.astype(BF16)


def _stick_weights(z, suffix, valid):
    a = jnp.exp(z + suffix)
    if valid is not None:
        a = jnp.where(valid, a, 0.0)
    return a


def _stick_block_lanes(z, valid, carry, ut):
    sums = _dot(_neg_log_stay(z, valid), ut)
    suffix = sums[:, 0:KEY_BLOCK] + carry
    return _stick_weights(z, suffix, valid), carry + sums[:, KEY_BLOCK:2 * KEY_BLOCK]


def _inproj_kernel(h_ref, nm_ref, win_ref, qn_ref, kn_ref, g_ref, cw_ref, cb_ref, pw_ref, ps_ref,
                   cst_ref, pst_ref,
                   q_ref, k_ref, v_ref, kb_ref, vb_ref, cp_ref, cso_ref, pso_ref,
                   uext, pext, *, nb, tm, sub, n_past, n_tiles):
    i = pl.program_id(1)

    @pl.when(i == 0)
    def _():
        uext[:, 0:CONV_HIST, :] = cst_ref[...]
        pext[:, 0:POOL_HIST, :] = pst_ref[...]

    def head_norm(t, gain):
        ms = _dot((t * t).astype(BF16), g_ref[...])
        return t * lax.rsqrt(ms + EPS) * gain

    rows = nb * sub
    for r0 in range(0, tm, sub):
        here = slice(r0, r0 + sub)
        x = h_ref[:, here, :].reshape(rows, D_MODEL)
        a = (_rms_scale(x) * nm_ref[...]).astype(BF16)
        z = _dot(a, win_ref[...])

        q = head_norm(z[:, 0:ATT_WIDTH], qn_ref[...])
        k = head_norm(z[:, ATT_WIDTH:2 * ATT_WIDTH], kn_ref[...])
        v = z[:, 2 * ATT_WIDTH:3 * ATT_WIDTH]
        o = 3 * ATT_WIDTH
        c_b = z[:, o:o + CONV_WIDTH].reshape(nb, sub, CONV_WIDTH)
        c_c = z[:, o + CONV_WIDTH:o + 2 * CONV_WIDTH]
        c_h = z[:, o + 2 * CONV_WIDTH:o + 3 * CONV_WIDTH]
        u_pool = z[:, o + 3 * CONV_WIDTH:o + 3 * CONV_WIDTH + POOL_WIDTH].reshape(nb, sub, POOL_WIDTH)

        q_ref[:, here, :] = (q * HEAD_DIM ** -0.5).astype(BF16).reshape(nb, sub, ATT_WIDTH)
        k_ref[:, here, :] = k.reshape(nb, sub, ATT_WIDTH)
        v_ref[:, here, :] = v.reshape(nb, sub, ATT_WIDTH)
        kb_ref[:, here, :] = k.astype(BF16).reshape(nb, sub, ATT_WIDTH)
        vb_ref[:, here, :] = v.astype(BF16).reshape(nb, sub, ATT_WIDTH)

        u = (c_c * c_h).reshape(nb, sub, CONV_WIDTH)
        uext[:, CONV_HIST + r0:CONV_HIST + r0 + sub, :] = u
        pext[:, POOL_HIST + r0:POOL_HIST + r0 + sub, :] = u_pool

        conv = cb_ref[...] + cw_ref[CONV_K - 1:CONV_K, :] * u
        for j in range(CONV_K - 1):
            first = CONV_HIST + r0 - (CONV_K - 1 - j)
            conv = conv + cw_ref[j:j + 1, :] * uext[:, first:first + sub, :]
        y_conv = c_b * conv

        pos = n_past + i * tm + r0 + lax.broadcasted_iota(jnp.int32, (nb, sub, V7X_LANES), 1)
        low_group = lax.broadcasted_iota(jnp.int32, (nb, sub, V7X_LANES), 2) < POOL_GROUP
        means = []
        for half in range(POOL_WIDTH // V7X_LANES):
            lanes = slice(half * V7X_LANES, (half + 1) * V7X_LANES)
            w_lo, w_hi = POOL_WINDOWS[2 * half], POOL_WINDOWS[2 * half + 1]
            acc = u_pool[:, :, lanes]
            sum_lo = None
            for back in range(1, w_hi):
                first = POOL_HIST + r0 - back
                acc = acc + pext[:, first:first + sub, lanes]
                if back == w_lo - 1:
                    sum_lo = acc
            win_sum = jnp.where(low_group, sum_lo, acc)
            count = jnp.minimum(pos + 1, jnp.where(low_group, w_lo, w_hi)).astype(F32)
            means.append(win_sum / count)
        mean = jnp.concatenate(means, axis=-1)
        d = (mean - u_pool).reshape(rows, POOL_WIDTH).astype(BF16)
        y_pool = (_dot(d, pw_ref[...]) * ps_ref[...]).reshape(nb, sub, POOL_WIDTH)

        cp_ref[:, here, 0:CONV_WIDTH] = y_conv.astype(BF16)
        cp_ref[:, here, CONV_WIDTH:CONV_WIDTH + POOL_WIDTH] = y_pool.astype(BF16)

    new_conv_hist = uext[:, tm:tm + CONV_HIST, :]
    new_pool_hist = pext[:, tm:tm + POOL_HIST, :]
    cso_ref[...] = new_conv_hist
    pso_ref[...] = new_pool_hist
    if n_tiles > 1:
        uext[:, 0:CONV_HIST, :] = new_conv_hist
        pext[:, 0:POOL_HIST, :] = new_pool_hist


def _inproj(h, layer, prm, conv_state, pool_state, *, n_past, nb, tm, sub):
    b, t, _ = h.shape
    n_tiles = t // tm
    grid = (b // nb, n_tiles)
    tok = lambda w: pl.BlockSpec((nb, tm, w), lambda bi, i: (bi, i, 0))
    lay = lambda *s: pl.BlockSpec((None,) + s, lambda bi, i: (layer,) + (0,) * len(s))
    hist = lambda r: pl.BlockSpec((nb, r, CONV_WIDTH), lambda bi, i: (bi, 0, 0))
    const = lambda *s: pl.BlockSpec(s, lambda bi, i: (0,) * len(s))
    out_shapes = (
        jax.ShapeDtypeStruct((b, t, ATT_WIDTH), BF16),
        jax.ShapeDtypeStruct((b, t, ATT_WIDTH), F32),
        jax.ShapeDtypeStruct((b, t, ATT_WIDTH), F32),
        jax.ShapeDtypeStruct((b, t, ATT_WIDTH), BF16),
        jax.ShapeDtypeStruct((b, t, ATT_WIDTH), BF16),
        jax.ShapeDtypeStruct((b, t, CONV_WIDTH + POOL_WIDTH), BF16),
        jax.ShapeDtypeStruct((b, CONV_HIST, CONV_WIDTH), F32),
        jax.ShapeDtypeStruct((b, POOL_HIST, POOL_WIDTH), F32),
    )
    return pl.pallas_call(
        functools.partial(_inproj_kernel, nb=nb, tm=tm, sub=sub, n_past=n_past, n_tiles=n_tiles),
        grid=grid,
        in_specs=[tok(D_MODEL), lay(1, D_MODEL), lay(D_MODEL, IN_WIDTH), lay(1, ATT_WIDTH), lay(1, ATT_WIDTH),
                  const(ATT_WIDTH, ATT_WIDTH), lay(CONV_K, CONV_WIDTH), lay(1, CONV_WIDTH),
                  lay(POOL_WIDTH, POOL_WIDTH), lay(1, POOL_WIDTH), hist(CONV_HIST), hist(POOL_HIST)],
        out_specs=(tok(ATT_WIDTH), tok(ATT_WIDTH), tok(ATT_WIDTH), tok(ATT_WIDTH), tok(ATT_WIDTH),
                   tok(CONV_WIDTH + POOL_WIDTH), hist(CONV_HIST), hist(POOL_HIST)),
        out_shape=out_shapes,
        scratch_shapes=[pltpu.VMEM((nb, CONV_HIST + tm, CONV_WIDTH), F32),
                        pltpu.VMEM((nb, POOL_HIST + tm, POOL_WIDTH), F32)],
        compiler_params=_compiler_params(("parallel", "arbitrary")),
        name="inproj",
    )(h, prm["norm_mix"], prm["w_in"], prm["q_norm"], prm["k_norm"], prm["head_mean"], prm["conv_w"],
      prm["conv_b"], prm["pool_w"], prm["pool_scale"], conv_state, pool_state)


def _prompt_attn_kernel(bias_ref, q_ref, k_ref, v_ref, u_ref, o_ref, vt_ref, qm_ref, carry_ref, acc_ref,
                        z_ref, s_ref, a_ref, *, layer, seq):
    qt = pl.program_id(1)
    n_kblocks = seq // KEY_BLOCK
    diag_blocks = QUERY_TILE // KEY_BLOCK

    @pl.when(qt == 0)
    def _():
        for kb in range(n_kblocks):
            vt_ref[kb] = v_ref[kb * KEY_BLOCK:(kb + 1) * KEY_BLOCK, :].T

    lane_head = lax.broadcasted_iota(jnp.int32, (QUERY_TILE, V7X_LANES), 1) // HEAD_DIM
    for h in range(N_HEADS):
        pair = slice(h // HEADS_PER_VREG * V7X_LANES, (h // HEADS_PER_VREG + 1) * V7X_LANES)
        q_pair = q_ref[:, pair]
        qm_ref[h] = jnp.where(lane_head == h % HEADS_PER_VREG, q_pair, jnp.zeros_like(q_pair))

    acc_ref[...] = jnp.zeros_like(acc_ref)
    u = u_ref[...]
    n_full = qt * diag_blocks
    last = n_full + diag_blocks - 1

    def logits(kb, slot, valid):
        start = pl.multiple_of(kb * KEY_BLOCK, KEY_BLOCK)
        for h in range(N_HEADS):
            pair = slice(h // HEADS_PER_VREG * V7X_LANES, (h // HEADS_PER_VREG + 1) * V7X_LANES)
            z = _dot_nt(k_ref[pl.ds(start, KEY_BLOCK), pair], qm_ref[h]) + bias_ref[layer, h]
            s_ref[slot, h] = _neg_log_stay(z, valid)
            z_ref[slot, h] = z if valid is None else jnp.where(valid, z, MASKED_LOGIT)

    def weights(slot, first):
        for h in range(N_HEADS):
            suffix = _dot(u, s_ref[slot, h])
            if not first:
                suffix = suffix + carry_ref[h]
            a_ref[slot, h] = jnp.exp(z_ref[slot, h] + suffix).astype(BF16)
            carry_ref[h] = suffix[0:1, :]

    def values(kb, slot):
        for h in range(N_HEADS):
            acc_ref[h] += _dot(vt_ref[kb, h * HEAD_DIM:(h + 1) * HEAD_DIM, :], a_ref[slot, h])

    key_row = lax.broadcasted_iota(jnp.int32, (KEY_BLOCK, QUERY_TILE), 0)
    query_col = lax.broadcasted_iota(jnp.int32, (KEY_BLOCK, QUERY_TILE), 1)
    assert diag_blocks == 2
    logits(last, 0, key_row + KEY_BLOCK < query_col)
    logits(last - 1, 1, key_row < query_col)
    weights(0, first=True)

    def body(j, _):
        newest = last - 2 * j
        logits(newest - 2, 0, None)
        weights(1, first=False)
        values(newest, 0)
        logits(newest - 3, 1, None)
        weights(0, first=False)
        values(newest - 1, 1)
        return 0

    lax.fori_loop(0, qt, body, 0)
    weights(1, first=False)
    values(1, 0)
    values(0, 1)
    for hp in range(N_HEADS // HEADS_PER_VREG):
        both = jnp.concatenate([acc_ref[hp * HEADS_PER_VREG + j] for j in range(HEADS_PER_VREG)], axis=0)
        o_ref[:, hp * V7X_LANES:(hp + 1) * V7X_LANES] = both.T.astype(BF16)


def _prompt_attention(q, kb, vb, sb_bias, u, layer):
    b, t, _ = q.shape
    seq_block = pl.BlockSpec((None, t, ATT_WIDTH), lambda bi, qt: (bi, 0, 0))
    tile = pl.BlockSpec((None, QUERY_TILE, ATT_WIDTH), lambda bi, qt: (bi, qt, 0))
    return pl.pallas_call(
        functools.partial(_prompt_attn_kernel, layer=layer, seq=t),
        grid=(b, t // QUERY_TILE),
        in_specs=[pl.BlockSpec(memory_space=pltpu.SMEM), tile, seq_block, seq_block,
                  pl.BlockSpec((KEY_BLOCK, KEY_BLOCK), lambda bi, qt: (0, 0))],
        out_specs=tile,
        out_shape=jax.ShapeDtypeStruct((b, t, ATT_WIDTH), BF16),
        scratch_shapes=[pltpu.VMEM((t // KEY_BLOCK, ATT_WIDTH, KEY_BLOCK), BF16),
                        pltpu.VMEM((N_HEADS, QUERY_TILE, V7X_LANES), BF16),
                        pltpu.VMEM((N_HEADS, 1, QUERY_TILE), F32),
                        pltpu.VMEM((N_HEADS, HEAD_DIM, QUERY_TILE), F32),
                        pltpu.VMEM((2, N_HEADS, KEY_BLOCK, QUERY_TILE), F32),
                        pltpu.VMEM((2, N_HEADS, KEY_BLOCK, QUERY_TILE), BF16),
                        pltpu.VMEM((2, N_HEADS, KEY_BLOCK, QUERY_TILE), BF16)],
        compiler_params=_compiler_params(("parallel", "arbitrary")),
        name="prompt_attn",
    )(sb_bias, q, kb, vb, u)


def _decode_begin(q_ref, kn_ref, vn_ref, bias, ut, qbd_ref, carry_ref, acc_ref, dec_seq):
    n_rows = N_HEADS * dec_seq
    row_head = lax.broadcasted_iota(jnp.int32, (n_rows, ATT_WIDTH), 0) // dec_seq
    lane_head = lax.broadcasted_iota(jnp.int32, (n_rows, ATT_WIDTH), 1) // HEAD_DIM
    q_rep = jnp.concatenate([q_ref[...].astype(F32)] * N_HEADS, axis=0)
    qbd_ref[...] = jnp.where(row_head == lane_head, q_rep, 0.0).astype(BF16)
    key_pos = lax.broadcasted_iota(jnp.int32, (n_rows, KEY_BLOCK), 1)
    query_t = lax.broadcasted_iota(jnp.int32, (n_rows, KEY_BLOCK), 0) % dec_seq
    z = _dot_nt(qbd_ref[...], kn_ref[...]) + bias
    a, carry = _stick_block_lanes(z, key_pos < query_t, jnp.zeros((n_rows, KEY_BLOCK), F32), ut)
    carry_ref[...] = carry
    acc_ref[...] = _dot(a.astype(BF16), vn_ref[...])


def _decode_pages(k_pages, v_pages, bias, ut, qbd_ref, carry_ref, acc_ref, z_ref, s_ref, a_ref):
    n_rows = qbd_ref.shape[0]
    for p, k_page in enumerate(k_pages):
        z = _dot(qbd_ref[...], k_page[...].astype(BF16)) + bias
        z_ref[p] = z
        s_ref[p * n_rows:(p + 1) * n_rows, :] = _neg_log_stay(z, None)
    carry = carry_ref[...]
    for g in range(0, len(k_pages), PAGES_PER_GROUP):
        sums = _dot(s_ref[g * n_rows:(g + PAGES_PER_GROUP) * n_rows, :], ut)
        for j in range(PAGES_PER_GROUP):
            rows = slice(j * n_rows, (j + 1) * n_rows)
            a_ref[g + j] = _stick_weights(z_ref[g + j], sums[rows, 0:KEY_BLOCK] + carry, None).astype(BF16)
            carry = carry + sums[rows, KEY_BLOCK:2 * KEY_BLOCK]
    carry_ref[...] = carry
    contrib = acc_ref[...]
    for p, v_page in enumerate(v_pages):
        contrib = contrib + _dot_nt(a_ref[p], v_page[...].astype(BF16))
    acc_ref[...] = contrib


def _decode_end(acc_ref, o_ref, dec_seq):
    lane_head = lax.broadcasted_iota(jnp.int32, (dec_seq, ATT_WIDTH), 1) // HEAD_DIM
    out = jnp.zeros((dec_seq, ATT_WIDTH), F32)
    for h in range(N_HEADS):
        out = out + jnp.where(lane_head == h, acc_ref[h * dec_seq:(h + 1) * dec_seq, :], 0.0)
    o_ref[...] = out.astype(BF16)


def _decode_attn_kernel(pt_ref, q_ref, kn_ref, vn_ref, bias_ref, ut_ref, *rest, dec_seq, n_chunks):
    k_pages = rest[:PAGES_PER_STEP]
    v_pages = rest[PAGES_PER_STEP:2 * PAGES_PER_STEP]
    o_ref = rest[2 * PAGES_PER_STEP]
    qbd_ref, carry_ref, acc_ref, z_ref, s_ref, a_ref = rest[2 * PAGES_PER_STEP + 1:]
    del pt_ref
    c = pl.program_id(1)
    ut = ut_ref[...]
    bias = bias_ref[...]

    @pl.when(c == 0)
    def _():
        _decode_begin(q_ref, kn_ref, vn_ref, bias, ut, qbd_ref, carry_ref, acc_ref, dec_seq)

    _decode_pages(k_pages, v_pages, bias, ut, qbd_ref, carry_ref, acc_ref, z_ref, s_ref, a_ref)

    @pl.when(c == n_chunks - 1)
    def _():
        _decode_end(acc_ref, o_ref, dec_seq)


def _decode_attention(q, k_new, v_new, cache_kt, cache_vt, page_table, bias_rows, ut, layer):
    db, ts, _ = q.shape
    n_pages = page_table.shape[1]
    n_chunks = n_pages // PAGES_PER_STEP
    n_rows = N_HEADS * ts
    assert n_chunks * PAGES_PER_STEP == n_pages and ts % V7X_SUBLANES == 0 and ts <= KEY_BLOCK

    def page_spec(p):
        def index(bi, c, pt):
            return (layer, pt[bi, n_pages - 1 - c * PAGES_PER_STEP - p], 0, 0)
        return pl.BlockSpec((None, None, ATT_WIDTH, PAGE_SIZE), index)

    row = lambda r: pl.BlockSpec((None, r, ATT_WIDTH), lambda bi, c, pt: (bi, 0, 0))
    grid_spec = pltpu.PrefetchScalarGridSpec(
        num_scalar_prefetch=1,
        grid=(db, n_chunks),
        in_specs=[row(ts), row(KEY_BLOCK), row(KEY_BLOCK),
                  pl.BlockSpec((None, n_rows, KEY_BLOCK), lambda bi, c, pt: (layer, 0, 0)),
                  pl.BlockSpec((KEY_BLOCK, 2 * KEY_BLOCK), lambda bi, c, pt: (0, 0))]
                 + [page_spec(p) for p in range(PAGES_PER_STEP)] * 2,
        out_specs=row(ts),
        scratch_shapes=[pltpu.VMEM((n_rows, ATT_WIDTH), BF16),
                        pltpu.VMEM((n_rows, KEY_BLOCK), F32),
                        pltpu.VMEM((n_rows, ATT_WIDTH), F32),
                        pltpu.VMEM((PAGES_PER_STEP, n_rows, KEY_BLOCK), F32),
                        pltpu.VMEM((PAGES_PER_STEP * n_rows, KEY_BLOCK), BF16),
                        pltpu.VMEM((PAGES_PER_STEP, n_rows, KEY_BLOCK), BF16)],
    )
    return pl.pallas_call(
        functools.partial(_decode_attn_kernel, dec_seq=ts, n_chunks=n_chunks),
        grid_spec=grid_spec,
        out_shape=jax.ShapeDtypeStruct((db, ts, ATT_WIDTH), BF16),
        compiler_params=_compiler_params(("parallel", "arbitrary")),
        name="decode_attn",
    )(page_table, q, k_new, v_new, bias_rows, ut, *([cache_kt] * PAGES_PER_STEP), *([cache_vt] * PAGES_PER_STEP))


def _post_kernel(h_ref, att_ref, cp_ref, p_ref, wo_ref, nf_ref, wu_ref, wd_ref, np_ref, wg_ref, wp_ref, o_ref):
    h = h_ref[...]
    h = h + _dot(att_ref[...], wo_ref[0:ATT_WIDTH, :]) + _dot(cp_ref[...], wo_ref[ATT_WIDTH:D_MODEL, :])
    m = (_rms_scale(h) * nf_ref[...]).astype(BF16)
    ff = jnp.zeros_like(h)
    for c in range(D_FF // FF_CHUNK):
        cols = slice(c * FF_CHUNK, (c + 1) * FF_CHUNK)
        up = jnp.square(jnp.maximum(_dot(m, wu_ref[:, cols]), 0.0)).astype(BF16)
        ff = ff + _dot(up, wd_ref[cols, :])
    h = h + ff
    g = _dot((_rms_scale(h) * np_ref[...]).astype(BF16), wg_ref[...])
    gate = 1.0 / (1.0 + jnp.exp(-g))
    o_ref[...] = h + gate * _dot(p_ref[...].astype(BF16), wp_ref[...])


def _post(h, att, cp, p_all, layer, prm, *, tm):
    n = h.shape[0]
    tok = lambda w: pl.BlockSpec((tm, w), lambda i: (i, 0))
    lay = lambda *s: pl.BlockSpec((None,) + s, lambda i: (layer,) + (0,) * len(s), pipeline_mode=pl.Buffered(1))
    return pl.pallas_call(
        _post_kernel,
        grid=(n // tm,),
        in_specs=[tok(D_MODEL), tok(ATT_WIDTH), tok(CONV_WIDTH + POOL_WIDTH),
                  pl.BlockSpec((None, tm, PLE_DIM), lambda i: (layer, i, 0)),
                  lay(D_MODEL, D_MODEL), lay(1, D_MODEL), lay(D_MODEL, D_FF), lay(D_FF, D_MODEL),
                  lay(1, D_MODEL), lay(D_MODEL, D_MODEL), lay(PLE_DIM, D_MODEL)],
        out_specs=tok(D_MODEL),
        out_shape=jax.ShapeDtypeStruct((n, D_MODEL), F32),
        compiler_params=_compiler_params(("parallel",)),
        name="post",
    )(h, att, cp, p_all, prm["w_out"], prm["norm_ffn"], prm["w_up"], prm["w_down"], prm["norm_ple"],
      prm["w_ple_gate"], prm["w_ple_proj"])


def _tile_rows(n, target):
    tm = min(n, target)
    assert n % tm == 0 and tm % V7X_SUBLANES == 0
    return tm


def kernel(x_prompt, x_sample, cache_k, cache_v, state_conv, state_pool, page_table, p_prompt, p_sample, norm_mix, w_in, q_norm, k_norm, sb_bias, conv_w, conv_b, pool_w, pool_scale, w_out, norm_ffn, w_up, w_down, norm_ple, w_ple_gate, w_ple_proj):
    depth = w_in.shape[0]
    b, t, _ = x_prompt.shape
    db, ts, _ = x_sample.shape
    n_phys = cache_k.shape[1]
    n_past = page_table.shape[1] * PAGE_SIZE
    n_groups = len(POOL_WINDOWS)

    row = lambda a: a.reshape(depth, 1, -1)
    group_eye = jnp.eye(n_groups, dtype=F32)
    head_eye = np.kron(np.eye(N_HEADS, dtype=np.float32), np.full((HEAD_DIM, HEAD_DIM), 1.0 / HEAD_DIM, np.float32))
    prm = {
        "norm_mix": row(norm_mix), "w_in": w_in.astype(BF16),
        "q_norm": row(jnp.tile(q_norm, (1, N_HEADS))), "k_norm": row(jnp.tile(k_norm, (1, N_HEADS))),
        "head_mean": jnp.asarray(head_eye, dtype=BF16),
        "conv_w": conv_w, "conv_b": row(conv_b),
        "pool_w": jnp.einsum("lgcd,gh->lgchd", pool_w, group_eye).reshape(depth, POOL_WIDTH, POOL_WIDTH).astype(BF16),
        "pool_scale": row(pool_scale),
        "w_out": w_out.astype(BF16), "norm_ffn": row(norm_ffn), "w_up": w_up.astype(BF16),
        "w_down": w_down.astype(BF16),
        "norm_ple": row(norm_ple), "w_ple_gate": w_ple_gate.astype(BF16),
        "w_ple_proj": w_ple_proj.astype(BF16),
    }
    u, ut = _suffix_sum_matrices()
    bias_rows = jnp.broadcast_to(jnp.repeat(sb_bias, ts, axis=1)[:, :, None], (depth, N_HEADS * ts, KEY_BLOCK))
    cache_kt = jnp.transpose(cache_k, (0, 1, 3, 4, 2)).reshape(depth, n_phys, ATT_WIDTH, PAGE_SIZE)
    cache_vt = jnp.transpose(cache_v, (0, 1, 3, 4, 2)).reshape(depth, n_phys, ATT_WIDTH, PAGE_SIZE)
    conv_hist_s = jnp.pad(state_conv, ((0, 0), (0, 0), (CONV_HIST - (CONV_K - 1), 0), (0, 0)))
    pool_hist_s = jnp.pad(state_pool, ((0, 0), (0, 0), (POOL_HIST - (POOL_MAX - 1), 0), (0, 0)))
    conv_hist_p = jnp.zeros((b, CONV_HIST, CONV_WIDTH), F32)
    pool_hist_p = jnp.zeros((b, POOL_HIST, POOL_WIDTH), F32)
    p_prompt = p_prompt.reshape(depth, b * t, PLE_DIM)
    p_sample = p_sample.reshape(depth, db * ts, PLE_DIM)

    tm_p = _tile_rows(t, 512)
    tm_in = _tile_rows(t, 2 * tm_p)
    hp, hs = x_prompt, x_sample
    outs = [[] for _ in range(8)]
    pad_rows = ((0, 0), (0, KEY_BLOCK - ts), (0, 0))
    for l in range(depth):
        q_s, k, v, kb_s, vb_s, cp_s, cso, pso = _inproj(hs, l, prm, conv_hist_s[l], pool_hist_s[l], n_past=n_past,
                                                        nb=db, tm=ts, sub=ts)
        for lst, val in zip(outs[4:], (k, v, cso, pso)):
            lst.append(val)

        q, k, v, kb, vb, cp, cso, pso = _inproj(hp, l, prm, conv_hist_p, pool_hist_p, n_past=0, nb=1, tm=tm_in,
                                                sub=tm_p)
        for lst, val in zip(outs[:4], (k, v, cso, pso)):
            lst.append(val)
        att = _prompt_attention(q, kb, vb, sb_bias, u, l)
        hp = _post(hp.reshape(b * t, D_MODEL), att.reshape(b * t, ATT_WIDTH), cp.reshape(b * t, -1), p_prompt, l,
                   prm, tm=tm_p).reshape(b, t, D_MODEL)
        att_s = _decode_attention(q_s, jnp.pad(kb_s, pad_rows), jnp.pad(vb_s, pad_rows), cache_kt, cache_vt,
                                  page_table, bias_rows, ut, l)
        hs = _post(hs.reshape(db * ts, D_MODEL), att_s.reshape(db * ts, ATT_WIDTH), cp_s.reshape(db * ts, -1),
                   p_sample, l, prm, tm=db * ts).reshape(db, ts, D_MODEL)

    def heads(xs):
        s = jnp.stack(xs)
        return s.reshape(s.shape[:-1] + (N_HEADS, HEAD_DIM))

    conv_tail = lambda xs: jnp.stack(xs)[:, :, CONV_HIST - (CONV_K - 1):, :]
    pool_tail = lambda xs: jnp.stack(xs)[:, :, POOL_HIST - (POOL_MAX - 1):, :]
    return (hp, hs,
            heads(outs[0]), heads(outs[1]), conv_tail(outs[2]), pool_tail(outs[3]),
            heads(outs[4]), heads(outs[5]), conv_tail(outs[6]), pool_tail(outs[7]))
```

```python
import functools

import jax
import jax.numpy as jnp
import numpy as np
from jax import lax
from jax.experimental import pallas as pl
from jax.experimental.pallas import tpu as pltpu

D_MODEL = 1024
N_HEADS = 8
HEAD_DIM = 64
ATT_WIDTH = N_HEADS * HEAD_DIM
CONV_WIDTH = 256
CONV_K = 3
POOL_WINDOWS = (2, 4, 8, 16)
POOL_GROUP = 64
POOL_WIDTH = 256
POOL_MAX = 16
IN_WIDTH = 3 * ATT_WIDTH + 3 * CONV_WIDTH + POOL_WIDTH
D_FF = 4 * D_MODEL
PLE_DIM = 256
PAGE_SIZE = 128
EPS = 1e-6

V7X_LANES = 128
V7X_SUBLANES = 8
V7X_VMEM_BYTES = 64 * 1024 * 1024
VMEM_LIMIT_BYTES = V7X_VMEM_BYTES * 7 // 8

CONV_HIST = V7X_SUBLANES
POOL_HIST = POOL_MAX
HEADS_PER_VREG = V7X_LANES // HEAD_DIM
FF_CHUNK = 1024
KEY_BLOCK = 128
QUERY_TILE = 256
FUSED_PAGES_PER_STEP = 16
PAGES_PER_GROUP = 4
MASKED_LOGIT = -1e30
LOG2_E = 1.4426950408889634

BF16 = jnp.bfloat16
F32 = jnp.float32


def _compiler_params(semantics):
    return pltpu.CompilerParams(dimension_semantics=semantics, vmem_limit_bytes=VMEM_LIMIT_BYTES)


def _rms_scale(x):
    return x * lax.rsqrt(jnp.mean(x * x, axis=-1, keepdims=True) + EPS)


def _dot(a, b):
    return jnp.dot(a, b, preferred_element_type=F32)


def _dot_nt(a, b):
    return lax.dot_general(a, b, (((1,), (1,)), ((), ())), preferred_element_type=F32)


def _softplus(z):
    return jnp.maximum(z, 0.0) + jnp.log(1.0 + jnp.exp2(jnp.abs(z) * -LOG2_E))


def _suffix_sum_matrices():
    r = np.arange(KEY_BLOCK)
    u = -(r[None, :] >= r[:, None]).astype(np.float32)
    lanes = np.concatenate([u.T, -np.ones_like(u)], axis=1)
    return jnp.asarray(u, dtype=BF16), jnp.asarray(lanes, dtype=BF16)


def _neg_log_stay(z, valid):
    s = _softplus(z)
    if valid is not None:
        s = jnp.where(valid, s, 0.0)
    return s.astype(BF16)


def _stick_weights(z, suffix, valid):
    a = jnp.exp(z + suffix)
    if valid is not None:
        a = jnp.where(valid, a, 0.0)
    return a


def _stick_block_lanes(z, valid, carry, ut):
    sums = _dot(_neg_log_stay(z, valid), ut)
    suffix = sums[:, 0:KEY_BLOCK] + carry
    return _stick_weights(z, suffix, valid), carry + sums[:, KEY_BLOCK:2 * KEY_BLOCK]


def _inproj_kernel(h_ref, nm_ref, win_ref, qn_ref, kn_ref, g_ref, cw_ref, cb_ref, pw_ref, ps_ref,
                   cst_ref, pst_ref,
                   q_ref, k_ref, v_ref, kb_ref, vb_ref, cp_ref, cso_ref, pso_ref,
                   uext, pext, *, nb, tm, sub, n_past, n_tiles):
    i = pl.program_id(1)

    @pl.when(i == 0)
    def _():
        uext[:, 0:CONV_HIST, :] = cst_ref[...]
        pext[:, 0:POOL_HIST, :] = pst_ref[...]

    def head_norm(t, gain):
        ms = _dot((t * t).astype(BF16), g_ref[...])
        return t * lax.rsqrt(ms + EPS) * gain

    rows = nb * sub
    for r0 in range(0, tm, sub):
        here = slice(r0, r0 + sub)
        x = h_ref[:, here, :].reshape(rows, D_MODEL)
        a = (_rms_scale(x) * nm_ref[...]).astype(BF16)
        z = _dot(a, win_ref[...])

        q = head_norm(z[:, 0:ATT_WIDTH], qn_ref[...])
        k = head_norm(z[:, ATT_WIDTH:2 * ATT_WIDTH], kn_ref[...])
        v = z[:, 2 * ATT_WIDTH:3 * ATT_WIDTH]
        o = 3 * ATT_WIDTH
        c_b = z[:, o:o + CONV_WIDTH].reshape(nb, sub, CONV_WIDTH)
        c_c = z[:, o + CONV_WIDTH:o + 2 * CONV_WIDTH]
        c_h = z[:, o + 2 * CONV_WIDTH:o + 3 * CONV_WIDTH]
        u_pool = z[:, o + 3 * CONV_WIDTH:o + 3 * CONV_WIDTH + POOL_WIDTH].reshape(nb, sub, POOL_WIDTH)

        q_ref[:, here, :] = (q * HEAD_DIM ** -0.5).astype(BF16).reshape(nb, sub, ATT_WIDTH)
        k_ref[:, here, :] = k.reshape(nb, sub, ATT_WIDTH)
        v_ref[:, here, :] = v.reshape(nb, sub, ATT_WIDTH)
        kb_ref[:, here, :] = k.astype(BF16).reshape(nb, sub, ATT_WIDTH)
        vb_ref[:, here, :] = v.astype(BF16).reshape(nb, sub, ATT_WIDTH)

        u = (c_c * c_h).reshape(nb, sub, CONV_WIDTH)
        uext[:, CONV_HIST + r0:CONV_HIST + r0 + sub, :] = u
        pext[:, POOL_HIST + r0:POOL_HIST + r0 + sub, :] = u_pool

        conv = cb_ref[...] + cw_ref[CONV_K - 1:CONV_K, :] * u
        for j in range(CONV_K - 1):
            first = CONV_HIST + r0 - (CONV_K - 1 - j)
            conv = conv + cw_ref[j:j + 1, :] * uext[:, first:first + sub, :]
        y_conv = c_b * conv

        pos = n_past + i * tm + r0 + lax.broadcasted_iota(jnp.int32, (nb, sub, V7X_LANES), 1)
        low_group = lax.broadcasted_iota(jnp.int32, (nb, sub, V7X_LANES), 2) < POOL_GROUP
        means = []
        for half in range(POOL_WIDTH // V7X_LANES):
            lanes = slice(half * V7X_LANES, (half + 1) * V7X_LANES)
            w_lo, w_hi = POOL_WINDOWS[2 * half], POOL_WINDOWS[2 * half + 1]
            acc = u_pool[:, :, lanes]
            sum_lo = None
            for back in range(1, w_hi):
                first = POOL_HIST + r0 - back
                acc = acc + pext[:, first:first + sub, lanes]
                if back == w_lo - 1:
                    sum_lo = acc
            win_sum = jnp.where(low_group, sum_lo, acc)
            count = jnp.minimum(pos + 1, jnp.where(low_group, w_lo, w_hi)).astype(F32)
            means.append(win_sum / count)
        mean = jnp.concatenate(means, axis=-1)
        d = (mean - u_pool).reshape(rows, POOL_WIDTH).astype(BF16)
        y_pool = (_dot(d, pw_ref[...]) * ps_ref[...]).reshape(nb, sub, POOL_WIDTH)

        cp_ref[:, here, 0:CONV_WIDTH] = y_conv.astype(BF16)
        cp_ref[:, here, CONV_WIDTH:CONV_WIDTH + POOL_WIDTH] = y_pool.astype(BF16)

    new_conv_hist = uext[:, tm:tm + CONV_HIST, :]
    new_pool_hist = pext[:, tm:tm + POOL_HIST, :]
    cso_ref[...] = new_conv_hist
    pso_ref[...] = new_pool_hist
    if n_tiles > 1:
        uext[:, 0:CONV_HIST, :] = new_conv_hist
        pext[:, 0:POOL_HIST, :] = new_pool_hist


def _inproj(h, layer, prm, conv_state, pool_state, *, n_past, nb, tm, sub):
    b, t, _ = h.shape
    n_tiles = t // tm
    grid = (b // nb, n_tiles)
    tok = lambda w: pl.BlockSpec((nb, tm, w), lambda bi, i: (bi, i, 0))
    lay = lambda *s: pl.BlockSpec((None,) + s, lambda bi, i: (layer,) + (0,) * len(s))
    hist = lambda r: pl.BlockSpec((nb, r, CONV_WIDTH), lambda bi, i: (bi, 0, 0))
    const = lambda *s: pl.BlockSpec(s, lambda bi, i: (0,) * len(s))
    out_shapes = (
        jax.ShapeDtypeStruct((b, t, ATT_WIDTH), BF16),
        jax.ShapeDtypeStruct((b, t, ATT_WIDTH), F32),
        jax.ShapeDtypeStruct((b, t, ATT_WIDTH), F32),
        jax.ShapeDtypeStruct((b, t, ATT_WIDTH), BF16),
        jax.ShapeDtypeStruct((b, t, ATT_WIDTH), BF16),
        jax.ShapeDtypeStruct((b, t, CONV_WIDTH + POOL_WIDTH), BF16),
        jax.ShapeDtypeStruct((b, CONV_HIST, CONV_WIDTH), F32),
        jax.ShapeDtypeStruct((b, POOL_HIST, POOL_WIDTH), F32),
    )
    return pl.pallas_call(
        functools.partial(_inproj_kernel, nb=nb, tm=tm, sub=sub, n_past=n_past, n_tiles=n_tiles),
        grid=grid,
        in_specs=[tok(D_MODEL), lay(1, D_MODEL), lay(D_MODEL, IN_WIDTH), lay(1, ATT_WIDTH), lay(1, ATT_WIDTH),
                  const(ATT_WIDTH, ATT_WIDTH), lay(CONV_K, CONV_WIDTH), lay(1, CONV_WIDTH),
                  lay(POOL_WIDTH, POOL_WIDTH), lay(1, POOL_WIDTH), hist(CONV_HIST), hist(POOL_HIST)],
        out_specs=(tok(ATT_WIDTH), tok(ATT_WIDTH), tok(ATT_WIDTH), tok(ATT_WIDTH), tok(ATT_WIDTH),
                   tok(CONV_WIDTH + POOL_WIDTH), hist(CONV_HIST), hist(POOL_HIST)),
        out_shape=out_shapes,
        scratch_shapes=[pltpu.VMEM((nb, CONV_HIST + tm, CONV_WIDTH), F32),
                        pltpu.VMEM((nb, POOL_HIST + tm, POOL_WIDTH), F32)],
        compiler_params=_compiler_params(("parallel", "arbitrary")),
        name="inproj",
    )(h, prm["norm_mix"], prm["w_in"], prm["q_norm"], prm["k_norm"], prm["head_mean"], prm["conv_w"],
      prm["conv_b"], prm["pool_w"], prm["pool_scale"], conv_state, pool_state)


def _prompt_attn_kernel(bias_ref, q_ref, k_ref, v_ref, u_ref, o_ref, vt_ref, qm_ref, carry_ref, acc_ref,
                        z_ref, s_ref, a_ref, *, layer, seq):
    qt = pl.program_id(1)
    n_kblocks = seq // KEY_BLOCK
    diag_blocks = QUERY_TILE // KEY_BLOCK

    @pl.when(qt == 0)
    def _():
        for kb in range(n_kblocks):
            vt_ref[kb] = v_ref[kb * KEY_BLOCK:(kb + 1) * KEY_BLOCK, :].T

    lane_head = lax.broadcasted_iota(jnp.int32, (QUERY_TILE, V7X_LANES), 1) // HEAD_DIM
    for h in range(N_HEADS):
        pair = slice(h // HEADS_PER_VREG * V7X_LANES, (h // HEADS_PER_VREG + 1) * V7X_LANES)
        q_pair = q_ref[:, pair]
        qm_ref[h] = jnp.where(lane_head == h % HEADS_PER_VREG, q_pair, jnp.zeros_like(q_pair))

    acc_ref[...] = jnp.zeros_like(acc_ref)
    u = u_ref[...]
    n_full = qt * diag_blocks
    last = n_full + diag_blocks - 1

    def logits(kb, slot, valid):
        start = pl.multiple_of(kb * KEY_BLOCK, KEY_BLOCK)
        for h in range(N_HEADS):
            pair = slice(h // HEADS_PER_VREG * V7X_LANES, (h // HEADS_PER_VREG + 1) * V7X_LANES)
            z = _dot_nt(k_ref[pl.ds(start, KEY_BLOCK), pair], qm_ref[h]) + bias_ref[layer, h]
            s_ref[slot, h] = _neg_log_stay(z, valid)
            z_ref[slot, h] = z if valid is None else jnp.where(valid, z, MASKED_LOGIT)

    def weights(slot, first):
        for h in range(N_HEADS):
            suffix = _dot(u, s_ref[slot, h])
            if not first:
                suffix = suffix + carry_ref[h]
            a_ref[slot, h] = jnp.exp(z_ref[slot, h] + suffix).astype(BF16)
            carry_ref[h] = suffix[0:1, :]

    def values(kb, slot):
        for h in range(N_HEADS):
            acc_ref[h] += _dot(vt_ref[kb, h * HEAD_DIM:(h + 1) * HEAD_DIM, :], a_ref[slot, h])

    key_row = lax.broadcasted_iota(jnp.int32, (KEY_BLOCK, QUERY_TILE), 0)
    query_col = lax.broadcasted_iota(jnp.int32, (KEY_BLOCK, QUERY_TILE), 1)
    assert diag_blocks == 2
    logits(last, 0, key_row + KEY_BLOCK < query_col)
    logits(last - 1, 1, key_row < query_col)
    weights(0, first=True)

    def body(j, _):
        newest = last - 2 * j
        logits(newest - 2, 0, None)
        weights(1, first=False)
        values(newest, 0)
        logits(newest - 3, 1, None)
        weights(0, first=False)
        values(newest - 1, 1)
        return 0

    lax.fori_loop(0, qt, body, 0)
    weights(1, first=False)
    values(1, 0)
    values(0, 1)
    for hp in range(N_HEADS // HEADS_PER_VREG):
        both = jnp.concatenate([acc_ref[hp * HEADS_PER_VREG + j] for j in range(HEADS_PER_VREG)], axis=0)
        o_ref[:, hp * V7X_LANES:(hp + 1) * V7X_LANES] = both.T.astype(BF16)


def _prompt_attention(q, kb, vb, sb_bias, u, layer):
    b, t, _ = q.shape
    seq_block = pl.BlockSpec((None, t, ATT_WIDTH), lambda bi, qt: (bi, 0, 0))
    tile = pl.BlockSpec((None, QUERY_TILE, ATT_WIDTH), lambda bi, qt: (bi, qt, 0))
    return pl.pallas_call(
        functools.partial(_prompt_attn_kernel, layer=layer, seq=t),
        grid=(b, t // QUERY_TILE),
        in_specs=[pl.BlockSpec(memory_space=pltpu.SMEM), tile, seq_block, seq_block,
                  pl.BlockSpec((KEY_BLOCK, KEY_BLOCK), lambda bi, qt: (0, 0))],
        out_specs=tile,
        out_shape=jax.ShapeDtypeStruct((b, t, ATT_WIDTH), BF16),
        scratch_shapes=[pltpu.VMEM((t // KEY_BLOCK, ATT_WIDTH, KEY_BLOCK), BF16),
                        pltpu.VMEM((N_HEADS, QUERY_TILE, V7X_LANES), BF16),
                        pltpu.VMEM((N_HEADS, 1, QUERY_TILE), F32),
                        pltpu.VMEM((N_HEADS, HEAD_DIM, QUERY_TILE), F32),
                        pltpu.VMEM((2, N_HEADS, KEY_BLOCK, QUERY_TILE), F32),
                        pltpu.VMEM((2, N_HEADS, KEY_BLOCK, QUERY_TILE), BF16),
                        pltpu.VMEM((2, N_HEADS, KEY_BLOCK, QUERY_TILE), BF16)],
        compiler_params=_compiler_params(("parallel", "arbitrary")),
        name="prompt_attn",
    )(sb_bias, q, kb, vb, u)


def _decode_begin(q_ref, kn_ref, vn_ref, bias, ut, qbd_ref, carry_ref, acc_ref, dec_seq):
    n_rows = N_HEADS * dec_seq
    row_head = lax.broadcasted_iota(jnp.int32, (n_rows, ATT_WIDTH), 0) // dec_seq
    lane_head = lax.broadcasted_iota(jnp.int32, (n_rows, ATT_WIDTH), 1) // HEAD_DIM
    q_rep = jnp.concatenate([q_ref[...].astype(F32)] * N_HEADS, axis=0)
    qbd_ref[...] = jnp.where(row_head == lane_head, q_rep, 0.0).astype(BF16)
    key_pos = lax.broadcasted_iota(jnp.int32, (n_rows, KEY_BLOCK), 1)
    query_t = lax.broadcasted_iota(jnp.int32, (n_rows, KEY_BLOCK), 0) % dec_seq
    z = _dot_nt(qbd_ref[...], kn_ref[...]) + bias
    a, carry = _stick_block_lanes(z, key_pos < query_t, jnp.zeros((n_rows, KEY_BLOCK), F32), ut)
    carry_ref[...] = carry
    acc_ref[...] = _dot(a.astype(BF16), vn_ref[...])


def _decode_pages(k_pages, v_pages, bias, ut, qbd_ref, carry_ref, acc_ref, z_ref, s_ref, a_ref):
    n_rows = qbd_ref.shape[0]
    for p, k_page in enumerate(k_pages):
        z = _dot(qbd_ref[...], k_page[...].astype(BF16)) + bias
        z_ref[p] = z
        s_ref[p * n_rows:(p + 1) * n_rows, :] = _neg_log_stay(z, None)
    carry = carry_ref[...]
    for g in range(0, len(k_pages), PAGES_PER_GROUP):
        sums = _dot(s_ref[g * n_rows:(g + PAGES_PER_GROUP) * n_rows, :], ut)
        for j in range(PAGES_PER_GROUP):
            rows = slice(j * n_rows, (j + 1) * n_rows)
            a_ref[g + j] = _stick_weights(z_ref[g + j], sums[rows, 0:KEY_BLOCK] + carry, None).astype(BF16)
            carry = carry + sums[rows, KEY_BLOCK:2 * KEY_BLOCK]
    carry_ref[...] = carry
    contrib = acc_ref[...]
    for p, v_page in enumerate(v_pages):
        contrib = contrib + _dot_nt(a_ref[p], v_page[...].astype(BF16))
    acc_ref[...] = contrib


def _decode_end(acc_ref, o_ref, dec_seq):
    lane_head = lax.broadcasted_iota(jnp.int32, (dec_seq, ATT_WIDTH), 1) // HEAD_DIM
    out = jnp.zeros((dec_seq, ATT_WIDTH), F32)
    for h in range(N_HEADS):
        out = out + jnp.where(lane_head == h, acc_ref[h * dec_seq:(h + 1) * dec_seq, :], 0.0)
    o_ref[...] = out.astype(BF16)


def _post_kernel(h_ref, att_ref, cp_ref, p_ref, wo_ref, nf_ref, wu_ref, wd_ref, np_ref, wg_ref, wp_ref, o_ref):
    h = h_ref[...]
    h = h + _dot(att_ref[...], wo_ref[0:ATT_WIDTH, :]) + _dot(cp_ref[...], wo_ref[ATT_WIDTH:D_MODEL, :])
    m = (_rms_scale(h) * nf_ref[...]).astype(BF16)
    ff = jnp.zeros_like(h)
    for c in range(D_FF // FF_CHUNK):
        cols = slice(c * FF_CHUNK, (c + 1) * FF_CHUNK)
        up = jnp.square(jnp.maximum(_dot(m, wu_ref[:, cols]), 0.0)).astype(BF16)
        ff = ff + _dot(up, wd_ref[cols, :])
    h = h + ff
    g = _dot((_rms_scale(h) * np_ref[...]).astype(BF16), wg_ref[...])
    gate = 1.0 / (1.0 + jnp.exp(-g))
    o_ref[...] = h + gate * _dot(p_ref[...].astype(BF16), wp_ref[...])


def _post(h, att, cp, p_all, layer, prm, *, tm):
    n = h.shape[0]
    tok = lambda w: pl.BlockSpec((tm, w), lambda i: (i, 0))
    lay = lambda *s: pl.BlockSpec((None,) + s, lambda i: (layer,) + (0,) * len(s), pipeline_mode=pl.Buffered(1))
    return pl.pallas_call(
        _post_kernel,
        grid=(n // tm,),
        in_specs=[tok(D_MODEL), tok(ATT_WIDTH), tok(CONV_WIDTH + POOL_WIDTH),
                  pl.BlockSpec((None, tm, PLE_DIM), lambda i: (layer, i, 0)),
                  lay(D_MODEL, D_MODEL), lay(1, D_MODEL), lay(D_MODEL, D_FF), lay(D_FF, D_MODEL),
                  lay(1, D_MODEL), lay(D_MODEL, D_MODEL), lay(PLE_DIM, D_MODEL)],
        out_specs=tok(D_MODEL),
        out_shape=jax.ShapeDtypeStruct((n, D_MODEL), F32),
        compiler_params=_compiler_params(("parallel",)),
        name="post",
    )(h, att, cp, p_all, prm["w_out"], prm["norm_ffn"], prm["w_up"], prm["w_down"], prm["norm_ple"],
      prm["w_ple_gate"], prm["w_ple_proj"])


def _post_decode_kernel(pt_ref, h_ref, att_ref, cp_ref, p_ref, wo_ref, nf_ref, wu_ref, wd_ref, np_ref, wg_ref,
                        wp_ref, q_ref, kn_ref, vn_ref, bias_ref, ut_ref, *rest, dec_seq, n_chunks):
    k_pages = rest[:FUSED_PAGES_PER_STEP]
    v_pages = rest[FUSED_PAGES_PER_STEP:2 * FUSED_PAGES_PER_STEP]
    o_ref, o_att_ref = rest[2 * FUSED_PAGES_PER_STEP:2 * FUSED_PAGES_PER_STEP + 2]
    hsum_ref, m_ref, qbd_ref, carry_ref, acc_ref, z_ref, s_ref, a_ref = rest[2 * FUSED_PAGES_PER_STEP + 2:]
    del pt_ref
    c = pl.program_id(1)
    ut = ut_ref[...]
    bias = bias_ref[...]

    @pl.when(c == 0)
    def _():
        h = h_ref[...]
        h = h + _dot(att_ref[...], wo_ref[0:ATT_WIDTH, :]) + _dot(cp_ref[...], wo_ref[ATT_WIDTH:D_MODEL, :])
        hsum_ref[...] = h
        m_ref[...] = (_rms_scale(h) * nf_ref[...]).astype(BF16)
        _decode_begin(q_ref, kn_ref, vn_ref, bias, ut, qbd_ref, carry_ref, acc_ref, dec_seq)

    up = jnp.square(jnp.maximum(_dot(m_ref[...], wu_ref[c]), 0.0)).astype(BF16)
    hsum_ref[...] += _dot(up, wd_ref[c])
    _decode_pages(k_pages, v_pages, bias, ut, qbd_ref, carry_ref, acc_ref, z_ref, s_ref, a_ref)

    @pl.when(c == n_chunks - 1)
    def _():
        h = hsum_ref[...]
        g = _dot((_rms_scale(h) * np_ref[...]).astype(BF16), wg_ref[...])
        gate = 1.0 / (1.0 + jnp.exp(-g))
        o_ref[...] = h + gate * _dot(p_ref[...].astype(BF16), wp_ref[...])
        _decode_end(acc_ref, o_att_ref, dec_seq)


def _post_decode(h, att, cp, p_all, layer, prm, q, k_new, v_new, cache_kt, cache_vt, page_table, bias_rows, ut, *, tm):
    n = h.shape[0]
    db, ts, _ = q.shape
    n_pages = page_table.shape[1]
    n_chunks = D_FF // FF_CHUNK
    n_rows = N_HEADS * ts
    assert n // tm == db and n_chunks * FUSED_PAGES_PER_STEP == n_pages
    assert ts % V7X_SUBLANES == 0 and ts <= KEY_BLOCK

    def page_spec(p):
        def index(i, c, pt):
            return (layer, pt[i, n_pages - 1 - c * FUSED_PAGES_PER_STEP - p], 0, 0)
        return pl.BlockSpec((None, None, ATT_WIDTH, PAGE_SIZE), index)

    once = dict(pipeline_mode=pl.Buffered(1))
    tok = lambda w, **kw: pl.BlockSpec((tm, w), lambda i, c, pt: (i, 0), **kw)
    lay = lambda *s: pl.BlockSpec((None,) + s, lambda i, c, pt: (layer,) + (0,) * len(s), **once)
    row = lambda r: pl.BlockSpec((None, r, ATT_WIDTH), lambda i, c, pt: (i, 0, 0))
    grid_spec = pltpu.PrefetchScalarGridSpec(
        num_scalar_prefetch=1,
        grid=(db, n_chunks),
        in_specs=[tok(D_MODEL), tok(ATT_WIDTH, **once), tok(CONV_WIDTH + POOL_WIDTH, **once),
                  pl.BlockSpec((None, tm, PLE_DIM), lambda i, c, pt: (layer, i, 0), **once),
                  lay(D_MODEL, D_MODEL), lay(1, D_MODEL), lay(n_chunks, D_MODEL, FF_CHUNK),
                  lay(n_chunks, FF_CHUNK, D_MODEL), lay(1, D_MODEL), lay(D_MODEL, D_MODEL), lay(PLE_DIM, D_MODEL),
                  row(ts), row(KEY_BLOCK), row(KEY_BLOCK),
                  pl.BlockSpec((None, n_rows, KEY_BLOCK), lambda i, c, pt: (layer, 0, 0)),
                  pl.BlockSpec((KEY_BLOCK, 2 * KEY_BLOCK), lambda i, c, pt: (0, 0))]
                 + [page_spec(p) for p in range(FUSED_PAGES_PER_STEP)] * 2,
        out_specs=(tok(D_MODEL), row(ts)),
        scratch_shapes=[pltpu.VMEM((tm, D_MODEL), F32),
                        pltpu.VMEM((tm, D_MODEL), BF16),
                        pltpu.VMEM((n_rows, ATT_WIDTH), BF16),
                        pltpu.VMEM((n_rows, KEY_BLOCK), F32),
                        pltpu.VMEM((n_rows, ATT_WIDTH), F32),
                        pltpu.VMEM((FUSED_PAGES_PER_STEP, n_rows, KEY_BLOCK), F32),
                        pltpu.VMEM((FUSED_PAGES_PER_STEP * n_rows, KEY_BLOCK), BF16),
                        pltpu.VMEM((FUSED_PAGES_PER_STEP, n_rows, KEY_BLOCK), BF16)],
    )
    return pl.pallas_call(
        functools.partial(_post_decode_kernel, dec_seq=ts, n_chunks=n_chunks),
        grid_spec=grid_spec,
        out_shape=(jax.ShapeDtypeStruct((n, D_MODEL), F32), jax.ShapeDtypeStruct((db, ts, ATT_WIDTH), BF16)),
        compiler_params=_compiler_params(("arbitrary", "arbitrary")),
        name="post_decode",
    )(page_table, h, att, cp, p_all, prm["w_out"], prm["norm_ffn"], prm["w_up_chunks"], prm["w_down_chunks"],
      prm["norm_ple"], prm["w_ple_gate"], prm["w_ple_proj"], q, k_new, v_new, bias_rows, ut,
      *([cache_kt] * FUSED_PAGES_PER_STEP), *([cache_vt] * FUSED_PAGES_PER_STEP))


def _tile_rows(n, target):
    tm = min(n, target)
    assert n % tm == 0 and tm % V7X_SUBLANES == 0
    return tm


def kernel(x_prompt, x_sample, cache_k, cache_v, state_conv, state_pool, page_table, p_prompt, p_sample, norm_mix, w_in, q_norm, k_norm, sb_bias, conv_w, conv_b, pool_w, pool_scale, w_out, norm_ffn, w_up, w_down, norm_ple, w_ple_gate, w_ple_proj):
    depth = w_in.shape[0]
    b, t, _ = x_prompt.shape
    db, ts, _ = x_sample.shape
    n_phys = cache_k.shape[1]
    n_past = page_table.shape[1] * PAGE_SIZE
    n_groups = len(POOL_WINDOWS)

    row = lambda a: a.reshape(depth, 1, -1)
    group_eye = jnp.eye(n_groups, dtype=F32)
    head_eye = np.kron(np.eye(N_HEADS, dtype=np.float32), np.full((HEAD_DIM, HEAD_DIM), 1.0 / HEAD_DIM, np.float32))
    prm = {
        "norm_mix": row(norm_mix), "w_in": w_in.astype(BF16),
        "q_norm": row(jnp.tile(q_norm, (1, N_HEADS))), "k_norm": row(jnp.tile(k_norm, (1, N_HEADS))),
        "head_mean": jnp.asarray(head_eye, dtype=BF16),
        "conv_w": conv_w, "conv_b": row(conv_b),
        "pool_w": jnp.einsum("lgcd,gh->lgchd", pool_w, group_eye).reshape(depth, POOL_WIDTH, POOL_WIDTH).astype(BF16),
        "pool_scale": row(pool_scale),
        "w_out": w_out.astype(BF16), "norm_ffn": row(norm_ffn), "w_up": w_up.astype(BF16),
        "w_down": w_down.astype(BF16),
        "w_up_chunks": jnp.swapaxes(w_up.astype(BF16).reshape(depth, D_MODEL, D_FF // FF_CHUNK, FF_CHUNK), 1, 2),
        "w_down_chunks": w_down.astype(BF16).reshape(depth, D_FF // FF_CHUNK, FF_CHUNK, D_MODEL),
        "norm_ple": row(norm_ple), "w_ple_gate": w_ple_gate.astype(BF16),
        "w_ple_proj": w_ple_proj.astype(BF16),
    }
    u, ut = _suffix_sum_matrices()
    bias_rows = jnp.broadcast_to(jnp.repeat(sb_bias, ts, axis=1)[:, :, None], (depth, N_HEADS * ts, KEY_BLOCK))
    cache_kt = jnp.transpose(cache_k, (0, 1, 3, 4, 2)).reshape(depth, n_phys, ATT_WIDTH, PAGE_SIZE)
    cache_vt = jnp.transpose(cache_v, (0, 1, 3, 4, 2)).reshape(depth, n_phys, ATT_WIDTH, PAGE_SIZE)
    conv_hist_s = jnp.pad(state_conv, ((0, 0), (0, 0), (CONV_HIST - (CONV_K - 1), 0), (0, 0)))
    pool_hist_s = jnp.pad(state_pool, ((0, 0), (0, 0), (POOL_HIST - (POOL_MAX - 1), 0), (0, 0)))
    conv_hist_p = jnp.zeros((b, CONV_HIST, CONV_WIDTH), F32)
    pool_hist_p = jnp.zeros((b, POOL_HIST, POOL_WIDTH), F32)
    p_prompt = p_prompt.reshape(depth, b * t, PLE_DIM)
    p_sample = p_sample.reshape(depth, db * ts, PLE_DIM)

    tm_p = _tile_rows(t, 512)
    tm_in = _tile_rows(t, 2 * tm_p)
    hp, hs = x_prompt, x_sample
    outs = [[] for _ in range(8)]
    pad_rows = ((0, 0), (0, KEY_BLOCK - ts), (0, 0))
    for l in range(depth):
        q_s, k, v, kb_s, vb_s, cp_s, cso, pso = _inproj(hs, l, prm, conv_hist_s[l], pool_hist_s[l], n_past=n_past,
                                                        nb=db, tm=ts, sub=ts)
        for lst, val in zip(outs[4:], (k, v, cso, pso)):
            lst.append(val)

        q, k, v, kb, vb, cp, cso, pso = _inproj(hp, l, prm, conv_hist_p, pool_hist_p, n_past=0, nb=1, tm=tm_in,
                                                sub=tm_p)
        for lst, val in zip(outs[:4], (k, v, cso, pso)):
            lst.append(val)
        att = _prompt_attention(q, kb, vb, sb_bias, u, l)
        hp, att_s = _post_decode(hp.reshape(b * t, D_MODEL), att.reshape(b * t, ATT_WIDTH), cp.reshape(b * t, -1),
                                 p_prompt, l, prm, q_s, jnp.pad(kb_s, pad_rows), jnp.pad(vb_s, pad_rows),
                                 cache_kt, cache_vt, page_table, bias_rows, ut, tm=tm_p)
        hp = hp.reshape(b, t, D_MODEL)
        hs = _post(hs.reshape(db * ts, D_MODEL), att_s.reshape(db * ts, ATT_WIDTH), cp_s.reshape(db * ts, -1),
                   p_sample, l, prm, tm=db * ts).reshape(db, ts, D_MODEL)

    def heads(xs):
        s = jnp.stack(xs)
        return s.reshape(s.shape[:-1] + (N_HEADS, HEAD_DIM))

    conv_tail = lambda xs: jnp.stack(xs)[:, :, CONV_HIST - (CONV_K - 1):, :]
    pool_tail = lambda xs: jnp.stack(xs)[:, :, POOL_HIST - (POOL_MAX - 1):, :]
    return (hp, hs,
            heads(outs[0]), heads(outs[1]), conv_tail(outs[2]), pool_tail(outs[3]),
            heads(outs[4]), heads(outs[5]), conv_tail(outs[6]), pool_tail(outs[7]))
```

```python
import functools

import jax
import jax.numpy as jnp
import numpy as np
from jax import lax
from jax.experimental import pallas as pl
from jax.experimental.pallas import tpu as pltpu

D_MODEL = 1024
N_HEADS = 8
HEAD_DIM = 64
ATT_WIDTH = N_HEADS * HEAD_DIM
CONV_WIDTH = 256
CONV_K = 3
POOL_WINDOWS = (2, 4, 8, 16)
POOL_GROUP = 64
POOL_WIDTH = 256
POOL_MAX = 16
IN_WIDTH = 3 * ATT_WIDTH + 3 * CONV_WIDTH + POOL_WIDTH
D_FF = 4 * D_MODEL
PLE_DIM = 256
PAGE_SIZE = 128
EPS = 1e-6

V7X_LANES = 128
V7X_SUBLANES = 8
V7X_VMEM_BYTES = 64 * 1024 * 1024
VMEM_LIMIT_BYTES = V7X_VMEM_BYTES * 7 // 8

CONV_HIST = V7X_SUBLANES
POOL_HIST = POOL_MAX
HEADS_PER_VREG = V7X_LANES // HEAD_DIM
FF_CHUNK = 1024
KEY_BLOCK = 128
QUERY_TILE = 256
FUSED_PAGES_PER_STEP = 16
PAGES_PER_GROUP = 4
MASKED_LOGIT = -1e30
LOG2_E = 1.4426950408889634

BF16 = jnp.bfloat16
F32 = jnp.float32


def _compiler_params(semantics):
    return pltpu.CompilerParams(dimension_semantics=semantics, vmem_limit_bytes=VMEM_LIMIT_BYTES)


def _rms_scale(x):
    return x * lax.rsqrt(jnp.mean(x * x, axis=-1, keepdims=True) + EPS)


def _dot(a, b):
    return jnp.dot(a, b, preferred_element_type=F32)


def _dot_nt(a, b):
    return lax.dot_general(a, b, (((1,), (1,)), ((), ())), preferred_element_type=F32)


def _softplus(z):
    return jnp.maximum(z, 0.0) + jnp.log(1.0 + jnp.exp2(jnp.abs(z) * -LOG2_E))


def _suffix_sum_matrices():
    r = np.arange(KEY_BLOCK)
    u = -(r[None, :] >= r[:, None]).astype(np.float32)
    lanes = np.concatenate([u.T, -np.ones_like(u)], axis=1)
    return jnp.asarray(u, dtype=BF16), jnp.asarray(lanes, dtype=BF16)


def _neg_log_stay(z, valid):
    s = _softplus(z)
    if valid is not None:
        s = jnp.where(valid, s, 0.0)
    return s.astype(BF16)


def _stick_weights(z, suffix, valid):
    a = jnp.exp(z + suffix)
    if valid is not None:
        a = jnp.where(valid, a, 0.0)
    return a


def _stick_block_lanes(z, valid, carry, ut):
    sums = _dot(_neg_log_stay(z, valid), ut)
    suffix = sums[:, 0:KEY_BLOCK] + carry
    return _stick_weights(z, suffix, valid), carry + sums[:, KEY_BLOCK:2 * KEY_BLOCK]


def _inproj_kernel(h_ref, nm_ref, win_ref, qn_ref, kn_ref, g_ref, cw_ref, cb_ref, pw_ref, ps_ref,
                   cst_ref, pst_ref, *rest, nb, tm, sub, n_past, n_tiles):
    q_ref, k_ref, v_ref, kb_ref, vb_ref, cp_ref, cso_ref, pso_ref, uext, pext = rest[-10:]
    i = pl.program_id(1)

    @pl.when(i == 0)
    def _():
        uext[:, 0:CONV_HIST, :] = cst_ref[...]
        pext[:, 0:POOL_HIST, :] = pst_ref[...]

    def head_norm(t, gain):
        ms = _dot((t * t).astype(BF16), g_ref[...])
        return t * lax.rsqrt(ms + EPS) * gain

    rows = nb * sub
    for r0 in range(0, tm, sub):
        here = slice(r0, r0 + sub)
        x = h_ref[:, here, :].reshape(rows, D_MODEL)
        a = (_rms_scale(x) * nm_ref[...]).astype(BF16)
        z = _dot(a, win_ref[...])

        q = head_norm(z[:, 0:ATT_WIDTH], qn_ref[...])
        k = head_norm(z[:, ATT_WIDTH:2 * ATT_WIDTH], kn_ref[...])
        v = z[:, 2 * ATT_WIDTH:3 * ATT_WIDTH]
        o = 3 * ATT_WIDTH
        c_b = z[:, o:o + CONV_WIDTH].reshape(nb, sub, CONV_WIDTH)
        c_c = z[:, o + CONV_WIDTH:o + 2 * CONV_WIDTH]
        c_h = z[:, o + 2 * CONV_WIDTH:o + 3 * CONV_WIDTH]
        u_pool = z[:, o + 3 * CONV_WIDTH:o + 3 * CONV_WIDTH + POOL_WIDTH].reshape(nb, sub, POOL_WIDTH)

        q_ref[:, here, :] = (q * HEAD_DIM ** -0.5).astype(BF16).reshape(nb, sub, ATT_WIDTH)
        k_ref[:, here, :] = k.reshape(nb, sub, ATT_WIDTH)
        v_ref[:, here, :] = v.reshape(nb, sub, ATT_WIDTH)
        kb_ref[:, here, :] = k.astype(BF16).reshape(nb, sub, ATT_WIDTH)
        vb_ref[:, here, :] = v.astype(BF16).reshape(nb, sub, ATT_WIDTH)

        u = (c_c * c_h).reshape(nb, sub, CONV_WIDTH)
        uext[:, CONV_HIST + r0:CONV_HIST + r0 + sub, :] = u
        pext[:, POOL_HIST + r0:POOL_HIST + r0 + sub, :] = u_pool

        conv = cb_ref[...] + cw_ref[CONV_K - 1:CONV_K, :] * u
        for j in range(CONV_K - 1):
            first = CONV_HIST + r0 - (CONV_K - 1 - j)
            conv = conv + cw_ref[j:j + 1, :] * uext[:, first:first + sub, :]
        y_conv = c_b * conv

        pos = n_past + i * tm + r0 + lax.broadcasted_iota(jnp.int32, (nb, sub, V7X_LANES), 1)
        low_group = lax.broadcasted_iota(jnp.int32, (nb, sub, V7X_LANES), 2) < POOL_GROUP
        means = []
        for half in range(POOL_WIDTH // V7X_LANES):
            lanes = slice(half * V7X_LANES, (half + 1) * V7X_LANES)
            w_lo, w_hi = POOL_WINDOWS[2 * half], POOL_WINDOWS[2 * half + 1]
            acc = u_pool[:, :, lanes]
            sum_lo = None
            for back in range(1, w_hi):
                first = POOL_HIST + r0 - back
                acc = acc + pext[:, first:first + sub, lanes]
                if back == w_lo - 1:
                    sum_lo = acc
            win_sum = jnp.where(low_group, sum_lo, acc)
            count = jnp.minimum(pos + 1, jnp.where(low_group, w_lo, w_hi)).astype(F32)
            means.append(win_sum / count)
        mean = jnp.concatenate(means, axis=-1)
        d = (mean - u_pool).reshape(rows, POOL_WIDTH).astype(BF16)
        y_pool = (_dot(d, pw_ref[...]) * ps_ref[...]).reshape(nb, sub, POOL_WIDTH)

        cp_ref[:, here, 0:CONV_WIDTH] = y_conv.astype(BF16)
        cp_ref[:, here, CONV_WIDTH:CONV_WIDTH + POOL_WIDTH] = y_pool.astype(BF16)

    new_conv_hist = uext[:, tm:tm + CONV_HIST, :]
    new_pool_hist = pext[:, tm:tm + POOL_HIST, :]
    cso_ref[...] = new_conv_hist
    pso_ref[...] = new_pool_hist
    if n_tiles > 1:
        uext[:, 0:CONV_HIST, :] = new_conv_hist
        pext[:, 0:POOL_HIST, :] = new_pool_hist


def _inproj(h, layer, prm, conv_state, pool_state, *, n_past, nb, tm, sub, kv_layers=None, kv_all=()):
    b, t, _ = h.shape
    n_tiles = t // tm
    grid = (b // nb, n_tiles)
    tok = lambda w: pl.BlockSpec((nb, tm, w), lambda bi, i: (bi, i, 0))
    lay = lambda *s: pl.BlockSpec((None,) + s, lambda bi, i: (layer,) + (0,) * len(s))
    hist = lambda r: pl.BlockSpec((nb, r, CONV_WIDTH), lambda bi, i: (bi, 0, 0))
    const = lambda *s: pl.BlockSpec(s, lambda bi, i: (0,) * len(s))
    if kv_layers is None:
        kv_shape, kv_spec = jax.ShapeDtypeStruct((b, t, ATT_WIDTH), F32), tok(ATT_WIDTH)
    else:
        kv_shape = jax.ShapeDtypeStruct((kv_layers, b, t, ATT_WIDTH), F32)
        kv_spec = pl.BlockSpec((None, nb, tm, ATT_WIDTH), lambda bi, i: (layer, bi, i, 0))
    n_inputs = 12
    out_shapes = (
        jax.ShapeDtypeStruct((b, t, ATT_WIDTH), BF16),
        kv_shape,
        kv_shape,
        jax.ShapeDtypeStruct((b, t, ATT_WIDTH), BF16),
        jax.ShapeDtypeStruct((b, t, ATT_WIDTH), BF16),
        jax.ShapeDtypeStruct((b, t, CONV_WIDTH + POOL_WIDTH), BF16),
        jax.ShapeDtypeStruct((b, CONV_HIST, CONV_WIDTH), F32),
        jax.ShapeDtypeStruct((b, POOL_HIST, POOL_WIDTH), F32),
    )
    return pl.pallas_call(
        functools.partial(_inproj_kernel, nb=nb, tm=tm, sub=sub, n_past=n_past, n_tiles=n_tiles),
        grid=grid,
        in_specs=[tok(D_MODEL), lay(1, D_MODEL), lay(D_MODEL, IN_WIDTH), lay(1, ATT_WIDTH), lay(1, ATT_WIDTH),
                  const(ATT_WIDTH, ATT_WIDTH), lay(CONV_K, CONV_WIDTH), lay(1, CONV_WIDTH),
                  lay(POOL_WIDTH, POOL_WIDTH), lay(1, POOL_WIDTH), hist(CONV_HIST), hist(POOL_HIST)]
                 + [pl.BlockSpec(memory_space=pl.ANY)] * len(kv_all),
        out_specs=(tok(ATT_WIDTH), kv_spec, kv_spec, tok(ATT_WIDTH), tok(ATT_WIDTH),
                   tok(CONV_WIDTH + POOL_WIDTH), hist(CONV_HIST), hist(POOL_HIST)),
        out_shape=out_shapes,
        scratch_shapes=[pltpu.VMEM((nb, CONV_HIST + tm, CONV_WIDTH), F32),
                        pltpu.VMEM((nb, POOL_HIST + tm, POOL_WIDTH), F32)],
        input_output_aliases={n_inputs + j: 1 + j for j in range(len(kv_all))},
        compiler_params=_compiler_params(("parallel", "arbitrary")),
        name="inproj",
    )(h, prm["norm_mix"], prm["w_in"], prm["q_norm"], prm["k_norm"], prm["head_mean"], prm["conv_w"],
      prm["conv_b"], prm["pool_w"], prm["pool_scale"], conv_state, pool_state, *kv_all)


def _prompt_attn_kernel(bias_ref, q_ref, k_ref, v_ref, u_ref, o_ref, vt_ref, qm_ref, carry_ref, acc_ref,
                        z_ref, s_ref, a_ref, *, layer, seq):
    qt = pl.program_id(1)
    n_kblocks = seq // KEY_BLOCK
    diag_blocks = QUERY_TILE // KEY_BLOCK

    @pl.when(qt == 0)
    def _():
        for kb in range(n_kblocks):
            vt_ref[kb] = v_ref[kb * KEY_BLOCK:(kb + 1) * KEY_BLOCK, :].T

    lane_head = lax.broadcasted_iota(jnp.int32, (QUERY_TILE, V7X_LANES), 1) // HEAD_DIM
    for h in range(N_HEADS):
        pair = slice(h // HEADS_PER_VREG * V7X_LANES, (h // HEADS_PER_VREG + 1) * V7X_LANES)
        q_pair = q_ref[:, pair]
        qm_ref[h] = jnp.where(lane_head == h % HEADS_PER_VREG, q_pair, jnp.zeros_like(q_pair))

    acc_ref[...] = jnp.zeros_like(acc_ref)
    u = u_ref[...]
    n_full = qt * diag_blocks
    last = n_full + diag_blocks - 1

    def logits(kb, slot, valid):
        start = pl.multiple_of(kb * KEY_BLOCK, KEY_BLOCK)
        for h in range(N_HEADS):
            pair = slice(h // HEADS_PER_VREG * V7X_LANES, (h // HEADS_PER_VREG + 1) * V7X_LANES)
            z = _dot_nt(k_ref[pl.ds(start, KEY_BLOCK), pair], qm_ref[h]) + bias_ref[layer, h]
            s_ref[slot, h] = _neg_log_stay(z, valid)
            z_ref[slot, h] = z if valid is None else jnp.where(valid, z, MASKED_LOGIT)

    def weights(slot, first):
        for h in range(N_HEADS):
            suffix = _dot(u, s_ref[slot, h])
            if not first:
                suffix = suffix + carry_ref[h]
            a_ref[slot, h] = jnp.exp(z_ref[slot, h] + suffix).astype(BF16)
            carry_ref[h] = suffix[0:1, :]

    def values(kb, slot):
        for h in range(N_HEADS):
            acc_ref[h] += _dot(vt_ref[kb, h * HEAD_DIM:(h + 1) * HEAD_DIM, :], a_ref[slot, h])

    key_row = lax.broadcasted_iota(jnp.int32, (KEY_BLOCK, QUERY_TILE), 0)
    query_col = lax.broadcasted_iota(jnp.int32, (KEY_BLOCK, QUERY_TILE), 1)
    assert diag_blocks == 2
    logits(last, 0, key_row + KEY_BLOCK < query_col)
    logits(last - 1, 1, key_row < query_col)
    weights(0, first=True)

    def body(j, _):
        newest = last - 2 * j
        logits(newest - 2, 0, None)
        weights(1, first=False)
        values(newest, 0)
        logits(newest - 3, 1, None)
        weights(0, first=False)
        values(newest - 1, 1)
        return 0

    lax.fori_loop(0, qt, body, 0)
    weights(1, first=False)
    values(1, 0)
    values(0, 1)
    for hp in range(N_HEADS // HEADS_PER_VREG):
        both = jnp.concatenate([acc_ref[hp * HEADS_PER_VREG + j] for j in range(HEADS_PER_VREG)], axis=0)
        o_ref[:, hp * V7X_LANES:(hp + 1) * V7X_LANES] = both.T.astype(BF16)


def _prompt_attention(q, kb, vb, sb_bias, u, layer):
    b, t, _ = q.shape
    seq_block = pl.BlockSpec((None, t, ATT_WIDTH), lambda bi, qt: (bi, 0, 0))
    tile = pl.BlockSpec((None, QUERY_TILE, ATT_WIDTH), lambda bi, qt: (bi, qt, 0))
    return pl.pallas_call(
        functools.partial(_prompt_attn_kernel, layer=layer, seq=t),
        grid=(b, t // QUERY_TILE),
        in_specs=[pl.BlockSpec(memory_space=pltpu.SMEM), tile, seq_block, seq_block,
                  pl.BlockSpec((KEY_BLOCK, KEY_BLOCK), lambda bi, qt: (0, 0))],
        out_specs=tile,
        out_shape=jax.ShapeDtypeStruct((b, t, ATT_WIDTH), BF16),
        scratch_shapes=[pltpu.VMEM((t // KEY_BLOCK, ATT_WIDTH, KEY_BLOCK), BF16),
                        pltpu.VMEM((N_HEADS, QUERY_TILE, V7X_LANES), BF16),
                        pltpu.VMEM((N_HEADS, 1, QUERY_TILE), F32),
                        pltpu.VMEM((N_HEADS, HEAD_DIM, QUERY_TILE), F32),
                        pltpu.VMEM((2, N_HEADS, KEY_BLOCK, QUERY_TILE), F32),
                        pltpu.VMEM((2, N_HEADS, KEY_BLOCK, QUERY_TILE), BF16),
                        pltpu.VMEM((2, N_HEADS, KEY_BLOCK, QUERY_TILE), BF16)],
        compiler_params=_compiler_params(("parallel", "arbitrary")),
        name="prompt_attn",
    )(sb_bias, q, kb, vb, u)


def _decode_begin(q_ref, kn_ref, vn_ref, bias, ut, qbd_ref, carry_ref, acc_ref, dec_seq):
    n_rows = N_HEADS * dec_seq
    row_head = lax.broadcasted_iota(jnp.int32, (n_rows, ATT_WIDTH), 0) // dec_seq
    lane_head = lax.broadcasted_iota(jnp.int32, (n_rows, ATT_WIDTH), 1) // HEAD_DIM
    q_rep = jnp.concatenate([q_ref[...].astype(F32)] * N_HEADS, axis=0)
    qbd_ref[...] = jnp.where(row_head == lane_head, q_rep, 0.0).astype(BF16)
    key_pos = lax.broadcasted_iota(jnp.int32, (n_rows, KEY_BLOCK), 1)
    query_t = lax.broadcasted_iota(jnp.int32, (n_rows, KEY_BLOCK), 0) % dec_seq
    z = _dot_nt(qbd_ref[...], kn_ref[...]) + bias
    a, carry = _stick_block_lanes(z, key_pos < query_t, jnp.zeros((n_rows, KEY_BLOCK), F32), ut)
    carry_ref[...] = carry
    acc_ref[...] = _dot(a.astype(BF16), vn_ref[...])


def _decode_pages(k_pages, v_pages, bias, ut, qbd_ref, carry_ref, acc_ref, z_ref, s_ref, a_ref):
    n_rows = qbd_ref.shape[0]
    for p, k_page in enumerate(k_pages):
        z = _dot(qbd_ref[...], k_page[...].astype(BF16)) + bias
        z_ref[p] = z
        s_ref[p * n_rows:(p + 1) * n_rows, :] = _neg_log_stay(z, None)
    carry = carry_ref[...]
    for g in range(0, len(k_pages), PAGES_PER_GROUP):
        sums = _dot(s_ref[g * n_rows:(g + PAGES_PER_GROUP) * n_rows, :], ut)
        for j in range(PAGES_PER_GROUP):
            rows = slice(j * n_rows, (j + 1) * n_rows)
            a_ref[g + j] = _stick_weights(z_ref[g + j], sums[rows, 0:KEY_BLOCK] + carry, None).astype(BF16)
            carry = carry + sums[rows, KEY_BLOCK:2 * KEY_BLOCK]
    carry_ref[...] = carry
    contrib = acc_ref[...]
    for p, v_page in enumerate(v_pages):
        contrib = contrib + _dot_nt(a_ref[p], v_page[...].astype(BF16))
    acc_ref[...] = contrib


def _decode_end(acc_ref, o_ref, dec_seq):
    lane_head = lax.broadcasted_iota(jnp.int32, (dec_seq, ATT_WIDTH), 1) // HEAD_DIM
    out = jnp.zeros((dec_seq, ATT_WIDTH), F32)
    for h in range(N_HEADS):
        out = out + jnp.where(lane_head == h, acc_ref[h * dec_seq:(h + 1) * dec_seq, :], 0.0)
    o_ref[...] = out.astype(BF16)


def _post_kernel(h_ref, att_ref, cp_ref, p_ref, wo_ref, nf_ref, wu_ref, wd_ref, np_ref, wg_ref, wp_ref, o_ref):
    h = h_ref[...]
    h = h + _dot(att_ref[...], wo_ref[0:ATT_WIDTH, :]) + _dot(cp_ref[...], wo_ref[ATT_WIDTH:D_MODEL, :])
    m = (_rms_scale(h) * nf_ref[...]).astype(BF16)
    ff = jnp.zeros_like(h)
    for c in range(D_FF // FF_CHUNK):
        cols = slice(c * FF_CHUNK, (c + 1) * FF_CHUNK)
        up = jnp.square(jnp.maximum(_dot(m, wu_ref[:, cols]), 0.0)).astype(BF16)
        ff = ff + _dot(up, wd_ref[cols, :])
    h = h + ff
    g = _dot((_rms_scale(h) * np_ref[...]).astype(BF16), wg_ref[...])
    gate = 1.0 / (1.0 + jnp.exp(-g))
    o_ref[...] = h + gate * _dot(p_ref[...].astype(BF16), wp_ref[...])


def _post(h, att, cp, p_all, layer, prm, *, tm):
    n = h.shape[0]
    tok = lambda w: pl.BlockSpec((tm, w), lambda i: (i, 0))
    lay = lambda *s: pl.BlockSpec((None,) + s, lambda i: (layer,) + (0,) * len(s), pipeline_mode=pl.Buffered(1))
    return pl.pallas_call(
        _post_kernel,
        grid=(n // tm,),
        in_specs=[tok(D_MODEL), tok(ATT_WIDTH), tok(CONV_WIDTH + POOL_WIDTH),
                  pl.BlockSpec((None, tm, PLE_DIM), lambda i: (layer, i, 0)),
                  lay(D_MODEL, D_MODEL), lay(1, D_MODEL), lay(D_MODEL, D_FF), lay(D_FF, D_MODEL),
                  lay(1, D_MODEL), lay(D_MODEL, D_MODEL), lay(PLE_DIM, D_MODEL)],
        out_specs=tok(D_MODEL),
        out_shape=jax.ShapeDtypeStruct((n, D_MODEL), F32),
        compiler_params=_compiler_params(("parallel",)),
        name="post",
    )(h, att, cp, p_all, prm["w_out"], prm["norm_ffn"], prm["w_up"], prm["w_down"], prm["norm_ple"],
      prm["w_ple_gate"], prm["w_ple_proj"])


def _post_decode_kernel(pt_ref, h_ref, att_ref, cp_ref, p_ref, wo_ref, nf_ref, wu_ref, wd_ref, np_ref, wg_ref,
                        wp_ref, q_ref, kn_ref, vn_ref, bias_ref, ut_ref, *rest, dec_seq, n_chunks):
    k_pages = rest[:FUSED_PAGES_PER_STEP]
    v_pages = rest[FUSED_PAGES_PER_STEP:2 * FUSED_PAGES_PER_STEP]
    o_ref, o_att_ref = rest[2 * FUSED_PAGES_PER_STEP:2 * FUSED_PAGES_PER_STEP + 2]
    hsum_ref, m_ref, qbd_ref, carry_ref, acc_ref, z_ref, s_ref, a_ref = rest[2 * FUSED_PAGES_PER_STEP + 2:]
    del pt_ref
    c = pl.program_id(1)
    ut = ut_ref[...]
    bias = bias_ref[...]

    @pl.when(c == 0)
    def _():
        h = h_ref[...]
        h = h + _dot(att_ref[...], wo_ref[0:ATT_WIDTH, :]) + _dot(cp_ref[...], wo_ref[ATT_WIDTH:D_MODEL, :])
        hsum_ref[...] = h
        m_ref[...] = (_rms_scale(h) * nf_ref[...]).astype(BF16)
        _decode_begin(q_ref, kn_ref, vn_ref, bias, ut, qbd_ref, carry_ref, acc_ref, dec_seq)

    up = jnp.square(jnp.maximum(_dot(m_ref[...], wu_ref[c]), 0.0)).astype(BF16)
    hsum_ref[...] += _dot(up, wd_ref[c])
    _decode_pages(k_pages, v_pages, bias, ut, qbd_ref, carry_ref, acc_ref, z_ref, s_ref, a_ref)

    @pl.when(c == n_chunks - 1)
    def _():
        h = hsum_ref[...]
        g = _dot((_rms_scale(h) * np_ref[...]).astype(BF16), wg_ref[...])
        gate = 1.0 / (1.0 + jnp.exp(-g))
        o_ref[...] = h + gate * _dot(p_ref[...].astype(BF16), wp_ref[...])
        _decode_end(acc_ref, o_att_ref, dec_seq)


def _post_decode(h, att, cp, p_all, layer, prm, q, k_new, v_new, cache_kt, cache_vt, page_table, bias_rows, ut, *, tm):
    n = h.shape[0]
    db, ts, _ = q.shape
    n_pages = page_table.shape[1]
    n_chunks = D_FF // FF_CHUNK
    n_rows = N_HEADS * ts
    assert n // tm == db and n_chunks * FUSED_PAGES_PER_STEP == n_pages
    assert ts % V7X_SUBLANES == 0 and ts <= KEY_BLOCK

    def page_spec(p):
        def index(i, c, pt):
            return (layer, pt[i, n_pages - 1 - c * FUSED_PAGES_PER_STEP - p], 0, 0)
        return pl.BlockSpec((None, None, ATT_WIDTH, PAGE_SIZE), index)

    once = dict(pipeline_mode=pl.Buffered(1))
    tok = lambda w, **kw: pl.BlockSpec((tm, w), lambda i, c, pt: (i, 0), **kw)
    lay = lambda *s: pl.BlockSpec((None,) + s, lambda i, c, pt: (layer,) + (0,) * len(s), **once)
    row = lambda r: pl.BlockSpec((None, r, ATT_WIDTH), lambda i, c, pt: (i, 0, 0))
    grid_spec = pltpu.PrefetchScalarGridSpec(
        num_scalar_prefetch=1,
        grid=(db, n_chunks),
        in_specs=[tok(D_MODEL), tok(ATT_WIDTH, **once), tok(CONV_WIDTH + POOL_WIDTH, **once),
                  pl.BlockSpec((None, tm, PLE_DIM), lambda i, c, pt: (layer, i, 0), **once),
                  lay(D_MODEL, D_MODEL), lay(1, D_MODEL), lay(n_chunks, D_MODEL, FF_CHUNK),
                  lay(n_chunks, FF_CHUNK, D_MODEL), lay(1, D_MODEL), lay(D_MODEL, D_MODEL), lay(PLE_DIM, D_MODEL),
                  row(ts), row(KEY_BLOCK), row(KEY_BLOCK),
                  pl.BlockSpec((None, n_rows, KEY_BLOCK), lambda i, c, pt: (layer, 0, 0)),
                  pl.BlockSpec((KEY_BLOCK, 2 * KEY_BLOCK), lambda i, c, pt: (0, 0))]
                 + [page_spec(p) for p in range(FUSED_PAGES_PER_STEP)] * 2,
        out_specs=(tok(D_MODEL), row(ts)),
        scratch_shapes=[pltpu.VMEM((tm, D_MODEL), F32),
                        pltpu.VMEM((tm, D_MODEL), BF16),
                        pltpu.VMEM((n_rows, ATT_WIDTH), BF16),
                        pltpu.VMEM((n_rows, KEY_BLOCK), F32),
                        pltpu.VMEM((n_rows, ATT_WIDTH), F32),
                        pltpu.VMEM((FUSED_PAGES_PER_STEP, n_rows, KEY_BLOCK), F32),
                        pltpu.VMEM((FUSED_PAGES_PER_STEP * n_rows, KEY_BLOCK), BF16),
                        pltpu.VMEM((FUSED_PAGES_PER_STEP, n_rows, KEY_BLOCK), BF16)],
    )
    return pl.pallas_call(
        functools.partial(_post_decode_kernel, dec_seq=ts, n_chunks=n_chunks),
        grid_spec=grid_spec,
        out_shape=(jax.ShapeDtypeStruct((n, D_MODEL), F32), jax.ShapeDtypeStruct((db, ts, ATT_WIDTH), BF16)),
        compiler_params=_compiler_params(("arbitrary", "arbitrary")),
        name="post_decode",
    )(page_table, h, att, cp, p_all, prm["w_out"], prm["norm_ffn"], prm["w_up_chunks"], prm["w_down_chunks"],
      prm["norm_ple"], prm["w_ple_gate"], prm["w_ple_proj"], q, k_new, v_new, bias_rows, ut,
      *([cache_kt] * FUSED_PAGES_PER_STEP), *([cache_vt] * FUSED_PAGES_PER_STEP))


def _tile_rows(n, target):
    tm = min(n, target)
    assert n % tm == 0 and tm % V7X_SUBLANES == 0
    return tm


def kernel(x_prompt, x_sample, cache_k, cache_v, state_conv, state_pool, page_table, p_prompt, p_sample, norm_mix, w_in, q_norm, k_norm, sb_bias, conv_w, conv_b, pool_w, pool_scale, w_out, norm_ffn, w_up, w_down, norm_ple, w_ple_gate, w_ple_proj):
    depth = w_in.shape[0]
    b, t, _ = x_prompt.shape
    db, ts, _ = x_sample.shape
    n_phys = cache_k.shape[1]
    n_past = page_table.shape[1] * PAGE_SIZE
    n_groups = len(POOL_WINDOWS)

    row = lambda a: a.reshape(depth, 1, -1)
    group_eye = jnp.eye(n_groups, dtype=F32)
    head_eye = np.kron(np.eye(N_HEADS, dtype=np.float32), np.full((HEAD_DIM, HEAD_DIM), 1.0 / HEAD_DIM, np.float32))
    prm = {
        "norm_mix": row(norm_mix), "w_in": w_in.astype(BF16),
        "q_norm": row(jnp.tile(q_norm, (1, N_HEADS))), "k_norm": row(jnp.tile(k_norm, (1, N_HEADS))),
        "head_mean": jnp.asarray(head_eye, dtype=BF16),
        "conv_w": conv_w, "conv_b": row(conv_b),
        "pool_w": jnp.einsum("lgcd,gh->lgchd", pool_w, group_eye).reshape(depth, POOL_WIDTH, POOL_WIDTH).astype(BF16),
        "pool_scale": row(pool_scale),
        "w_out": w_out.astype(BF16), "norm_ffn": row(norm_ffn), "w_up": w_up.astype(BF16),
        "w_down": w_down.astype(BF16),
        "w_up_chunks": jnp.swapaxes(w_up.astype(BF16).reshape(depth, D_MODEL, D_FF // FF_CHUNK, FF_CHUNK), 1, 2),
        "w_down_chunks": w_down.astype(BF16).reshape(depth, D_FF // FF_CHUNK, FF_CHUNK, D_MODEL),
        "norm_ple": row(norm_ple), "w_ple_gate": w_ple_gate.astype(BF16),
        "w_ple_proj": w_ple_proj.astype(BF16),
    }
    u, ut = _suffix_sum_matrices()
    bias_rows = jnp.broadcast_to(jnp.repeat(sb_bias, ts, axis=1)[:, :, None], (depth, N_HEADS * ts, KEY_BLOCK))
    cache_kt = jnp.transpose(cache_k, (0, 1, 3, 4, 2)).reshape(depth, n_phys, ATT_WIDTH, PAGE_SIZE)
    cache_vt = jnp.transpose(cache_v, (0, 1, 3, 4, 2)).reshape(depth, n_phys, ATT_WIDTH, PAGE_SIZE)
    conv_hist_s = jnp.pad(state_conv, ((0, 0), (0, 0), (CONV_HIST - (CONV_K - 1), 0), (0, 0)))
    pool_hist_s = jnp.pad(state_pool, ((0, 0), (0, 0), (POOL_HIST - (POOL_MAX - 1), 0), (0, 0)))
    conv_hist_p = jnp.zeros((b, CONV_HIST, CONV_WIDTH), F32)
    pool_hist_p = jnp.zeros((b, POOL_HIST, POOL_WIDTH), F32)
    p_prompt = p_prompt.reshape(depth, b * t, PLE_DIM)
    p_sample = p_sample.reshape(depth, db * ts, PLE_DIM)

    tm_p = _tile_rows(t, 512)
    tm_in = _tile_rows(t, 2 * tm_p)
    hp, hs = x_prompt, x_sample
    outs = [[] for _ in range(8)]
    kv_prompt = ()
    pad_rows = ((0, 0), (0, KEY_BLOCK - ts), (0, 0))
    for l in range(depth):
        q_s, k, v, kb_s, vb_s, cp_s, cso, pso = _inproj(hs, l, prm, conv_hist_s[l], pool_hist_s[l], n_past=n_past,
                                                        nb=db, tm=ts, sub=ts)
        for lst, val in zip(outs[4:], (k, v, cso, pso)):
            lst.append(val)

        q, k_prompt, v_prompt, kb, vb, cp, cso, pso = _inproj(hp, l, prm, conv_hist_p, pool_hist_p, n_past=0, nb=1,
                                                              tm=tm_in, sub=tm_p, kv_layers=depth, kv_all=kv_prompt)
        kv_prompt = (k_prompt, v_prompt)
        for lst, val in zip(outs[2:4], (cso, pso)):
            lst.append(val)
        att = _prompt_attention(q, kb, vb, sb_bias, u, l)
        hp, att_s = _post_decode(hp.reshape(b * t, D_MODEL), att.reshape(b * t, ATT_WIDTH), cp.reshape(b * t, -1),
                                 p_prompt, l, prm, q_s, jnp.pad(kb_s, pad_rows), jnp.pad(vb_s, pad_rows),
                                 cache_kt, cache_vt, page_table, bias_rows, ut, tm=tm_p)
        hp = hp.reshape(b, t, D_MODEL)
        hs = _post(hs.reshape(db * ts, D_MODEL), att_s.reshape(db * ts, ATT_WIDTH), cp_s.reshape(db * ts, -1),
                   p_sample, l, prm, tm=db * ts).reshape(db, ts, D_MODEL)

    heads = lambda s: s.reshape(s.shape[:-1] + (N_HEADS, HEAD_DIM))
    conv_tail = lambda xs: jnp.stack(xs)[:, :, CONV_HIST - (CONV_K - 1):, :]
    pool_tail = lambda xs: jnp.stack(xs)[:, :, POOL_HIST - (POOL_MAX - 1):, :]
    return (hp, hs,
            heads(kv_prompt[0]), heads(kv_prompt[1]), conv_tail(outs[2]), pool_tail(outs[3]),
            heads(jnp.stack(outs[4])), heads(jnp.stack(outs[5])), conv_tail(outs[6]), pool_tail(outs[7]))
```
